```python
import jax, jax.numpy as jnp
from jax import lax
import numpy as np

D_MODEL = 1024
BATCH = 4
SEQ = 8192
DEPTH = 2
DEC_BATCH = 8
DEC_SEQ = 16
PAST_LEN = 2048

CHUNK = 64
HEAD_DIM = 64
N_Q_HEADS = 16
N_KV_HEADS = 4
GROUP = N_Q_HEADS // N_KV_HEADS
ATTN_WIDTH = N_Q_HEADS * HEAD_DIM
KV_WIDTH = N_KV_HEADS * HEAD_DIM
WINDOW = 128
W_CHUNKS = WINDOW // CHUNK
BAND = (W_CHUNKS + 1) * CHUNK
CONV_CH = D_MODEL
CONV_W = 31
D_FF = 4 * D_MODEL
N_BRANCH = 2
N_IN = ATTN_WIDTH + 2 * KV_WIDTH + 2 * CONV_CH + N_BRANCH * D_MODEL
SPLITS = (ATTN_WIDTH, ATTN_WIDTH + KV_WIDTH, ATTN_WIDTH + 2 * KV_WIDTH,
          ATTN_WIDTH + 2 * KV_WIDTH + 2 * CONV_CH)
ROPE_THETA = 10000.0
EPS = 1e-6
NEG_INF = -1e30

kernel_name = "hybrid_swa_sink_conformer_conv_streaming_step"


def rms_norm(x, g):
    xf = x.astype(jnp.float32)
    y = xf * lax.rsqrt(jnp.mean(xf * xf, axis=-1, keepdims=True) + EPS)
    return (y * g.astype(jnp.float32)).astype(x.dtype)


def layer_norm(x, g, b):
    xf = x.astype(jnp.float32)
    mu = jnp.mean(xf, axis=-1, keepdims=True)
    var = jnp.mean(jnp.square(xf - mu), axis=-1, keepdims=True)
    y = (xf - mu) * lax.rsqrt(var + EPS)
    return (y * g.astype(jnp.float32) + b.astype(jnp.float32)).astype(x.dtype)


def rotary(x, pos):
    half = HEAD_DIM // 2
    inv_freq = ROPE_THETA ** (-jnp.arange(half, dtype=jnp.float32) / half)
    ang = pos.astype(jnp.float32)[:, None] * inv_freq[None, :]
    cos = jnp.cos(ang)[:, None, :]
    sin = jnp.sin(ang)[:, None, :]
    xf = x.astype(jnp.float32)
    x1, x2 = xf[..., :half], xf[..., half:]
    out = jnp.concatenate([x1 * cos - x2 * sin, x2 * cos + x1 * sin], axis=-1)
    return out.astype(x.dtype)


def softmax_with_sink(s, sink_b):
    m = jnp.maximum(jnp.max(s, axis=-1, keepdims=True), sink_b)
    p = jnp.exp(s - m)
    return p / (jnp.sum(p, axis=-1, keepdims=True) + jnp.exp(sink_b - m))


def attn_prompt(q, k, v, sink):
    B, T = q.shape[0], q.shape[1]
    nc = T // CHUNK
    qb = q.reshape(B, nc, CHUNK, N_KV_HEADS, GROUP, HEAD_DIM)
    pad = ((0, 0), (WINDOW, 0), (0, 0), (0, 0))
    kp = jnp.pad(k, pad).reshape(B, nc + W_CHUNKS, CHUNK, N_KV_HEADS, HEAD_DIM)
    vp = jnp.pad(v, pad).reshape(B, nc + W_CHUNKS, CHUNK, N_KV_HEADS, HEAD_DIM)
    kb = jnp.concatenate([kp[:, i:i + nc] for i in range(W_CHUNKS + 1)], axis=2)
    vb = jnp.concatenate([vp[:, i:i + nc] for i in range(W_CHUNKS + 1)], axis=2)
    s = jnp.einsum('bnqkgd,bnskd->bnkgqs', qb, kb).astype(jnp.float32) * (HEAD_DIM ** -0.5)
    key_chunk = jnp.arange(nc)[:, None] - W_CHUNKS + (jnp.arange(BAND) // CHUNK)[None, :]
    valid = (key_chunk >= 0)[None, :, None, None, None, :]
    s = jnp.where(valid, s, NEG_INF)
    sink_b = sink.astype(jnp.float32).reshape(1, 1, N_KV_HEADS, GROUP, 1, 1)
    p = softmax_with_sink(s, sink_b).astype(v.dtype)
    o = jnp.einsum('bnkgqs,bnskd->bnqkgd', p, vb)
    return o.reshape(B, T, ATTN_WIDTH)


def attn_sample(q, k, v, k_past, v_past, sink):
    B, T = q.shape[0], q.shape[1]
    qs = q.reshape(B, T, N_KV_HEADS, GROUP, HEAD_DIM)
    kk = jnp.concatenate([k_past.astype(k.dtype), k], axis=1)
    vv = jnp.concatenate([v_past.astype(v.dtype), v], axis=1)
    s = jnp.einsum('bqkgd,bskd->bkgqs', qs, kk).astype(jnp.float32) * (HEAD_DIM ** -0.5)
    sink_b = sink.astype(jnp.float32).reshape(1, N_KV_HEADS, GROUP, 1, 1)
    p = softmax_with_sink(s, sink_b).astype(v.dtype)
    o = jnp.einsum('bkgqs,bskd->bqkgd', p, vv)
    return o.reshape(B, T, ATTN_WIDTH)


def conv_module(u, conv_past, w_dw, b_dw, ln_g, ln_b, w_pw2):
    a = u[..., :CONV_CH] * jax.nn.sigmoid(u[..., CONV_CH:])
    full = jnp.concatenate([conv_past.astype(a.dtype), a], axis=1)
    y = lax.conv_general_dilated(full, w_dw[:, None, :].astype(a.dtype), window_strides=(1,),
                                 padding='VALID', dimension_numbers=('NWC', 'WIO', 'NWC'),
                                 feature_group_count=CONV_CH) + b_dw
    y = jax.nn.silu(layer_norm(y, ln_g, ln_b))
    return y @ w_pw2, full[:, -(CONV_W - 1):]


def trunk_layer(x, pos, k_past, v_past, conv_past, norm_mix, w_in, sink, w_o_attn, w_dw, b_dw,
                ln_g, ln_b, w_pw2, w_out, norm_mlp, w_up, w_down):
    B, T = x.shape[0], x.shape[1]
    h = rms_norm(x, norm_mix)
    q, k, v, u, gl = jnp.split(h @ w_in, SPLITS, axis=-1)
    q = rotary(q.reshape(B, T, N_Q_HEADS, HEAD_DIM), pos)
    k = rotary(k.reshape(B, T, N_KV_HEADS, HEAD_DIM), pos)
    v = v.reshape(B, T, N_KV_HEADS, HEAD_DIM)
    if k_past is None:
        attn = attn_prompt(q, k, v, sink)
        k_keep, v_keep = k[:, -WINDOW:], v[:, -WINDOW:]
        conv_past = jnp.zeros((B, CONV_W - 1, CONV_CH), x.dtype)
    else:
        attn = attn_sample(q, k, v, k_past, v_past, sink)
        k_keep, v_keep = k, v
    conv_out, conv_keep = conv_module(u, conv_past, w_dw, b_dw, ln_g, ln_b, w_pw2)
    g = jax.nn.sigmoid(gl)
    merged = g[..., :D_MODEL] * (attn @ w_o_attn) + g[..., D_MODEL:] * conv_out
    x = x + merged @ w_out
    hm = rms_norm(x, norm_mlp)
    x = x + jnp.square(jax.nn.relu(hm @ w_up)) @ w_down
    return x, k_keep, v_keep, conv_keep


def setup_inputs(seed: int = 0) -> dict:
    key = jax.random.key(seed)
    ks = jax.random.split(key, 20)
    f32 = jnp.float32
    nrm = lambda k, shape, scale: jax.random.normal(k, shape, f32) * scale
    return {
        "x_prompt": nrm(ks[0], (BATCH, SEQ, D_MODEL), 1.0),
        "x_sample": nrm(ks[1], (DEC_BATCH, DEC_SEQ, D_MODEL), 1.0),
        "cache_k": nrm(ks[2], (DEPTH, DEC_BATCH, WINDOW, N_KV_HEADS, HEAD_DIM), 1.0),
        "cache_v": nrm(ks[3], (DEPTH, DEC_BATCH, WINDOW, N_KV_HEADS, HEAD_DIM), 1.0),
        "state_conv": nrm(ks[4], (DEPTH, DEC_BATCH, CONV_W - 1, CONV_CH), 0.5),
        "norm_mix": 1.0 + nrm(ks[5], (DEPTH, D_MODEL), 0.02),
        "w_in": nrm(ks[6], (DEPTH, D_MODEL, N_IN), D_MODEL ** -0.5),
        "sinks": nrm(ks[7], (DEPTH, N_Q_HEADS), 0.5),
        "w_o_attn": nrm(ks[8], (DEPTH, ATTN_WIDTH, D_MODEL), ATTN_WIDTH ** -0.5),
        "w_dw": nrm(ks[9], (DEPTH, CONV_W, CONV_CH), CONV_W ** -0.5),
        "b_dw": nrm(ks[10], (DEPTH, CONV_CH), 0.02),
        "ln_conv_g": 1.0 + nrm(ks[11], (DEPTH, CONV_CH), 0.02),
        "ln_conv_b": nrm(ks[12], (DEPTH, CONV_CH), 0.02),
        "w_pw2": nrm(ks[13], (DEPTH, CONV_CH, D_MODEL), CONV_CH ** -0.5),
        "w_out": nrm(ks[14], (DEPTH, D_MODEL, D_MODEL), D_MODEL ** -0.5),
        "norm_mlp": 1.0 + nrm(ks[15], (DEPTH, D_MODEL), 0.02),
        "w_up": nrm(ks[16], (DEPTH, D_MODEL, D_FF), D_MODEL ** -0.5),
        "w_down": nrm(ks[17], (DEPTH, D_FF, D_MODEL), D_FF ** -0.5),
        "norm_final": 1.0 + nrm(ks[18], (D_MODEL,), 0.02),
    }


def reference(x_prompt, x_sample, cache_k, cache_v, state_conv, norm_mix, w_in, sinks, w_o_attn,
              w_dw, b_dw, ln_conv_g, ln_conv_b, w_pw2, w_out, norm_mlp, w_up, w_down, norm_final):
    pos_p = jnp.arange(x_prompt.shape[1], dtype=jnp.int32)
    pos_s = PAST_LEN + jnp.arange(x_sample.shape[1], dtype=jnp.int32)
    xp, xs = x_prompt, x_sample
    kp_l, vp_l, cp_l, ks_l, vs_l, cs_l = [], [], [], [], [], []
    for l in range(DEPTH):
        w = (norm_mix[l], w_in[l], sinks[l], w_o_attn[l], w_dw[l], b_dw[l], ln_conv_g[l],
             ln_conv_b[l], w_pw2[l], w_out[l], norm_mlp[l], w_up[l], w_down[l])
        xp, kp, vp, cp = trunk_layer(xp, pos_p, None, None, None, *w)
        xs, kn, vn, cn = trunk_layer(xs, pos_s, cache_k[l], cache_v[l], state_conv[l], *w)
        kp_l.append(kp); vp_l.append(vp); cp_l.append(cp)
        ks_l.append(kn); vs_l.append(vn); cs_l.append(cn)
    y_prompt = rms_norm(xp, norm_final)
    y_sample = rms_norm(xs, norm_final)
    k_prompt_new = jnp.stack(kp_l)
    v_prompt_new = jnp.stack(vp_l)
    conv_prompt_new = jnp.stack(cp_l)
    k_sample_new = jnp.stack(ks_l)
    v_sample_new = jnp.stack(vs_l)
    conv_sample_new = jnp.stack(cs_l)
    return (y_prompt, y_sample, k_prompt_new, v_prompt_new, conv_prompt_new, k_sample_new, v_sample_new, conv_sample_new)
```

```python
import functools

import jax
import jax.numpy as jnp
from jax import lax
from jax.experimental import pallas as pl
from jax.experimental.pallas import tpu as pltpu

D_MODEL = 1024
DEPTH = 2
PAST_LEN = 2048
CHUNK = 64
HEAD_DIM = 64
N_Q_HEADS = 16
N_KV_HEADS = 4
GROUP = N_Q_HEADS // N_KV_HEADS
KV_WIDTH = N_KV_HEADS * HEAD_DIM
WINDOW = 128
CONV_W = 31
CONV_KEEP = CONV_W - 1
D_FF = 4 * D_MODEL
ROPE_THETA = 10000.0
EPS = 1e-6
NEG_INF = -1e30

LANES = 128
CONV_HIST = 32
CONV_PAD = CONV_HIST - CONV_KEEP
VMEM_LIMIT = 56 * 1024 * 1024

PROMPT_TILE = 256
FFN_TILE = 512

F32 = jnp.float32
BF16 = jnp.bfloat16


def _rms(x, g):
    return x * lax.rsqrt(jnp.mean(x * x, axis=-1, keepdims=True) + EPS) * g


def _dot(a, b):
    return jnp.dot(a, b, preferred_element_type=F32)


def _rope(x, cos, sin_lo, sin_hi):
    outs = []
    for b in range(x.shape[1] // LANES):
        xb = x[:, b * LANES:(b + 1) * LANES]
        up = pltpu.roll(xb, LANES - HEAD_DIM // 2, 1)
        dn = pltpu.roll(xb, HEAD_DIM // 2, 1)
        outs.append(xb * cos + up * sin_lo + dn * sin_hi)
    return jnp.concatenate(outs, axis=1)


def _attend(q, k, v, sink_ref, valid):
    nq = q.shape[0]
    lane_head = lax.broadcasted_iota(jnp.int32, (nq, KV_WIDTH), 1) // HEAD_DIM
    out = [None] * GROUP
    for i in range(N_KV_HEADS):
        sel = lane_head == i
        qm = jnp.concatenate(
            [jnp.where(sel, q[:, j * KV_WIDTH:(j + 1) * KV_WIDTH], jnp.zeros((), q.dtype))
             for j in range(GROUP)], axis=0)
        s = lax.dot_general(qm, k, (((1,), (1,)), ((), ())), preferred_element_type=F32)
        if valid is not None:
            s = jnp.where(valid, s, NEG_INF)
        ps = []
        for j in range(GROUP):
            sj = s[j * nq:(j + 1) * nq]
            sink = sink_ref[GROUP * i + j]
            m = jnp.maximum(jnp.max(sj, axis=-1, keepdims=True), sink)
            p = jnp.exp(sj - m)
            den = jnp.sum(p, axis=-1, keepdims=True) + jnp.exp(sink - m)
            ps.append((p * (1.0 / den)).astype(BF16))
        o = _dot(jnp.concatenate(ps, axis=0), v)
        for j in range(GROUP):
            oj = o[j * nq:(j + 1) * nq]
            out[j] = oj if i == 0 else jnp.where(sel, oj, out[j])
    return jnp.concatenate(out, axis=1)


def _dwconv(full_ref, row0, nrows, wdw_ref, y_ref, yrow0, row_block, lane_block):
    for r in range(0, nrows, row_block):
        for c in range(0, D_MODEL, lane_block):
            acc = jnp.zeros((row_block, lane_block), F32)
            for j in range(CONV_W):
                acc = acc + (full_ref[row0 + r + j:row0 + r + j + row_block, c:c + lane_block]
                             * wdw_ref[j:j + 1, c:c + lane_block])
            y_ref[yrow0 + r:yrow0 + r + row_block, c:c + lane_block] = acc


def _conv_tail(y, bdw, lng, lnb):
    y = y + bdw
    mu = jnp.mean(y, axis=-1, keepdims=True)
    yc = y - mu
    var = jnp.mean(yc * yc, axis=-1, keepdims=True)
    z = yc * lax.rsqrt(var + EPS) * lng + lnb
    return z * jax.nn.sigmoid(z)


def _merge_out(x, h, attn_bf16, conv_act_bf16, wg_ref, wo_ref, wpw2_ref, wout_ref):
    g = jax.nn.sigmoid(_dot(h, wg_ref[...]))
    merged = (g[:, :D_MODEL] * _dot(attn_bf16, wo_ref[...])
              + g[:, D_MODEL:] * _dot(conv_act_bf16, wpw2_ref[...]))
    return x + _dot(merged.astype(BF16), wout_ref[...])


def _mixer_prompt_kernel(sink_ref, x_ref, cq_ref, slq_ref, shq_ref, ck_ref, slk_ref, shk_ref,
                         nmix_ref, wq_ref, wk_ref, wv_ref, wu_ref, wg_ref, wo_ref, wdw_ref,
                         bdw_ref, lng_ref, lnb_ref, wpw2_ref, wout_ref,
                         xo_ref, kkeep_ref, vkeep_ref, ckeep_ref,
                         kh_ref, vh_ref, afull_ref, q_ref, at_ref, y_ref, *, tile):
    t = pl.program_id(1)
    last = t == pl.num_programs(1) - 1

    @pl.when(t == 0)
    def _():
        kh_ref[0:WINDOW, :] = jnp.zeros((WINDOW, KV_WIDTH), BF16)
        vh_ref[0:WINDOW, :] = jnp.zeros((WINDOW, KV_WIDTH), BF16)
        afull_ref[0:CONV_HIST, :] = jnp.zeros((CONV_HIST, D_MODEL), F32)

    x = x_ref[0]
    h = _rms(x, nmix_ref[...]).astype(BF16)

    q_ref[...] = _rope(_dot(h, wq_ref[...]), cq_ref[...], slq_ref[...], shq_ref[...]).astype(BF16)
    k = _rope(_dot(h, wk_ref[...]), ck_ref[...], slk_ref[...], shk_ref[...])
    v = _dot(h, wv_ref[...])
    kh_ref[WINDOW:WINDOW + tile, :] = k.astype(BF16)
    vh_ref[WINDOW:WINDOW + tile, :] = v.astype(BF16)

    @pl.when(last)
    def _():
        kkeep_ref[0] = k[tile - WINDOW:, :]
        vkeep_ref[0] = v[tile - WINDOW:, :]

    band = WINDOW + CHUNK
    key_chunk = lax.broadcasted_iota(jnp.int32, (1, band), 1) // CHUNK

    def chunk_body(c, carry):
        r0 = pl.multiple_of(c * CHUNK, CHUNK)
        valid = (t * (tile // CHUNK) + c - WINDOW // CHUNK + key_chunk) >= 0
        o = _attend(q_ref[pl.ds(r0, CHUNK), :], kh_ref[pl.ds(r0, band), :],
                    vh_ref[pl.ds(r0, band), :], sink_ref, valid)
        at_ref[pl.ds(r0, CHUNK), :] = o.astype(BF16)
        return carry

    lax.fori_loop(0, tile // CHUNK, chunk_body, 0)
    kh_ref[0:WINDOW, :] = kh_ref[tile:tile + WINDOW, :]
    vh_ref[0:WINDOW, :] = vh_ref[tile:tile + WINDOW, :]

    u = _dot(h, wu_ref[...])
    afull_ref[CONV_HIST:CONV_HIST + tile, :] = u[:, :D_MODEL] * jax.nn.sigmoid(u[:, D_MODEL:])

    @pl.when(last)
    def _():
        ckeep_ref[0] = afull_ref[CONV_HIST + tile - CONV_KEEP:CONV_HIST + tile, :]

    _dwconv(afull_ref, CONV_PAD, tile, wdw_ref, y_ref, 0, 32, 256)
    afull_ref[0:CONV_HIST, :] = afull_ref[tile:tile + CONV_HIST, :]
    conv_act = _conv_tail(y_ref[...], bdw_ref[...], lng_ref[...], lnb_ref[...]).astype(BF16)

    xo_ref[0] = _merge_out(x, h, at_ref[...], conv_act, wg_ref, wo_ref, wpw2_ref, wout_ref)


def _mixer_sample_kernel(sink_ref, x_ref, cq_ref, slq_ref, shq_ref, ck_ref, slk_ref, shk_ref,
                         kpast_ref, vpast_ref, cpast_ref,
                         nmix_ref, wq_ref, wk_ref, wv_ref, wu_ref, wg_ref, wo_ref, wdw_ref,
                         bdw_ref, lng_ref, lnb_ref, wpw2_ref, wout_ref,
                         xo_ref, knew_ref, vnew_ref, ckeep_ref,
                         kf_ref, vf_ref, afull_ref, at_ref, y_ref, *, streams, steps):
    x = x_ref[...]
    h = _rms(x, nmix_ref[...]).astype(BF16)
    q = _rope(_dot(h, wq_ref[...]), cq_ref[...], slq_ref[...], shq_ref[...]).astype(BF16)
    k = _rope(_dot(h, wk_ref[...]), ck_ref[...], slk_ref[...], shk_ref[...])
    v = _dot(h, wv_ref[...])
    knew_ref[...] = k
    vnew_ref[...] = v
    u = _dot(h, wu_ref[...])
    a = u[:, :D_MODEL] * jax.nn.sigmoid(u[:, D_MODEL:])

    for s in range(streams):
        rows = slice(s * steps, (s + 1) * steps)
        kf_ref[0:WINDOW, :] = kpast_ref[s].astype(BF16)
        vf_ref[0:WINDOW, :] = vpast_ref[s].astype(BF16)
        kf_ref[WINDOW:WINDOW + steps, :] = k[rows].astype(BF16)
        vf_ref[WINDOW:WINDOW + steps, :] = v[rows].astype(BF16)
        at_ref[rows, :] = _attend(q[rows], kf_ref[...], vf_ref[...], sink_ref, None).astype(BF16)

        afull_ref[0:CONV_HIST, :] = cpast_ref[s]
        afull_ref[CONV_HIST:CONV_HIST + steps, :] = a[rows]
        ckeep_ref[s] = afull_ref[CONV_HIST + steps - CONV_KEEP:CONV_HIST + steps, :]
        _dwconv(afull_ref, CONV_PAD, steps, wdw_ref, y_ref, s * steps, steps, D_MODEL)

    conv_act = _conv_tail(y_ref[...], bdw_ref[...], lng_ref[...], lnb_ref[...]).astype(BF16)
    xo_ref[...] = _merge_out(x, h, at_ref[...], conv_act, wg_ref, wo_ref, wpw2_ref, wout_ref)


def _ffn_kernel(x_ref, nmlp_ref, wup_ref, wdown_ref, nfin_ref, o_ref, *, final):
    x = x_ref[...]
    hm = _rms(x, nmlp_ref[...]).astype(BF16)
    acc = x
    for c in range(0, D_FF, D_MODEL):
        r = jnp.maximum(_dot(hm, wup_ref[:, c:c + D_MODEL]), 0.0)
        acc = acc + _dot((r * r).astype(BF16), wdown_ref[c:c + D_MODEL, :])
    o_ref[...] = _rms(acc, nfin_ref[...]) if final else acc


def _const_spec(shape):
    nd = len(shape)
    return pl.BlockSpec(shape, lambda *_: (0,) * nd, pipeline_mode=pl.Buffered(1))


_SMEM_SPEC = pl.BlockSpec(memory_space=pltpu.SMEM)


def _mixer_weight_specs():
    return [
        _const_spec((1, D_MODEL)),
        _const_spec((D_MODEL, D_MODEL)),
        _const_spec((D_MODEL, KV_WIDTH)),
        _const_spec((D_MODEL, KV_WIDTH)),
        _const_spec((D_MODEL, 2 * D_MODEL)),
        _const_spec((D_MODEL, 2 * D_MODEL)),
        _const_spec((D_MODEL, D_MODEL)),
        _const_spec((CONV_W, D_MODEL)),
        _const_spec((1, D_MODEL)),
        _const_spec((1, D_MODEL)),
        _const_spec((1, D_MODEL)),
        _const_spec((D_MODEL, D_MODEL)),
        _const_spec((D_MODEL, D_MODEL)),
    ]


def _mixer_prompt(x, tabs, sink, weights):
    batch, seq, _ = x.shape
    tile = PROMPT_TILE
    row_spec = pl.BlockSpec((1, tile, D_MODEL), lambda b, t: (b, t, 0))
    tab_spec = pl.BlockSpec((tile, LANES), lambda b, t: (t, 0))
    keep_spec = lambda rows, width: pl.BlockSpec((1, rows, width), lambda b, t: (b, 0, 0))
    return pl.pallas_call(
        functools.partial(_mixer_prompt_kernel, tile=tile),
        grid=(batch, seq // tile),
        in_specs=[_SMEM_SPEC, row_spec] + [tab_spec] * 6 + _mixer_weight_specs(),
        out_specs=[row_spec, keep_spec(WINDOW, KV_WIDTH), keep_spec(WINDOW, KV_WIDTH),
                   keep_spec(CONV_KEEP, D_MODEL)],
        out_shape=[jax.ShapeDtypeStruct((batch, seq, D_MODEL), F32),
                   jax.ShapeDtypeStruct((batch, WINDOW, KV_WIDTH), F32),
                   jax.ShapeDtypeStruct((batch, WINDOW, KV_WIDTH), F32),
                   jax.ShapeDtypeStruct((batch, CONV_KEEP, D_MODEL), F32)],
        scratch_shapes=[pltpu.VMEM((WINDOW + tile, KV_WIDTH), BF16),
                        pltpu.VMEM((WINDOW + tile, KV_WIDTH), BF16),
                        pltpu.VMEM((CONV_HIST + tile, D_MODEL), F32),
                        pltpu.VMEM((tile, D_MODEL), BF16),
                        pltpu.VMEM((tile, D_MODEL), BF16),
                        pltpu.VMEM((tile, D_MODEL), F32)],
        compiler_params=pltpu.CompilerParams(
            dimension_semantics=("arbitrary", "arbitrary"), vmem_limit_bytes=VMEM_LIMIT),
        name="mixer_prompt",
    )(sink, x, *tabs, *weights)


def _mixer_sample(x, tabs, kpast, vpast, cpast, sink, weights):
    streams, steps, _ = x.shape
    rows = streams * steps
    full = lambda shape: pl.BlockSpec(shape, lambda i: (0,) * len(shape))
    return pl.pallas_call(
        functools.partial(_mixer_sample_kernel, streams=streams, steps=steps),
        grid=(1,),
        in_specs=[_SMEM_SPEC, full((rows, D_MODEL))] + [full((rows, LANES))] * 6
        + [full((streams, WINDOW, KV_WIDTH)), full((streams, WINDOW, KV_WIDTH)),
           full((streams, CONV_HIST, D_MODEL))] + _mixer_weight_specs(),
        out_specs=[full((rows, D_MODEL)), full((rows, KV_WIDTH)), full((rows, KV_WIDTH)),
                   full((streams, CONV_KEEP, D_MODEL))],
        out_shape=[jax.ShapeDtypeStruct((rows, D_MODEL), F32),
                   jax.ShapeDtypeStruct((rows, KV_WIDTH), F32),
                   jax.ShapeDtypeStruct((rows, KV_WIDTH), F32),
                   jax.ShapeDtypeStruct((streams, CONV_KEEP, D_MODEL), F32)],
        scratch_shapes=[pltpu.VMEM((WINDOW + steps, KV_WIDTH), BF16),
                        pltpu.VMEM((WINDOW + steps, KV_WIDTH), BF16),
                        pltpu.VMEM((CONV_HIST + steps, D_MODEL), F32),
                        pltpu.VMEM((rows, D_MODEL), BF16),
                        pltpu.VMEM((rows, D_MODEL), F32)],
        compiler_params=pltpu.CompilerParams(
            dimension_semantics=("arbitrary",), vmem_limit_bytes=VMEM_LIMIT),
        name="mixer_sample",
    )(sink, x.reshape(rows, D_MODEL), *tabs, kpast, vpast, cpast, *weights)


def _ffn(x, nmlp, wup, wdown, nfin, final, name):
    rows = x.shape[0]
    tile = min(FFN_TILE, rows)
    row_spec = pl.BlockSpec((tile, D_MODEL), lambda i: (i, 0))
    return pl.pallas_call(
        functools.partial(_ffn_kernel, final=final),
        grid=(rows // tile,),
        in_specs=[row_spec, _const_spec((1, D_MODEL)), _const_spec((D_MODEL, D_FF)),
                  _const_spec((D_FF, D_MODEL)), _const_spec((1, D_MODEL))],
        out_specs=row_spec,
        out_shape=jax.ShapeDtypeStruct((rows, D_MODEL), F32),
        compiler_params=pltpu.CompilerParams(
            dimension_semantics=("arbitrary",), vmem_limit_bytes=VMEM_LIMIT),
        name=name,
    )(x, nmlp, wup, wdown, nfin)


def _rope_tables(pos):
    half = HEAD_DIM // 2
    inv_freq = ROPE_THETA ** (-jnp.arange(half, dtype=F32) / half)
    ang = pos.astype(F32)[:, None] * inv_freq[None, :]
    cos, sin = jnp.cos(ang), jnp.sin(ang)
    zero = jnp.zeros_like(sin)
    reps = LANES // HEAD_DIM
    cos_t = jnp.tile(jnp.concatenate([cos, cos], axis=1), (1, reps))
    sin_lo = jnp.tile(jnp.concatenate([-sin, zero], axis=1), (1, reps))
    sin_hi = jnp.tile(jnp.concatenate([zero, sin], axis=1), (1, reps))
    scale = HEAD_DIM ** -0.5
    return (cos_t * scale, sin_lo * scale, sin_hi * scale, cos_t, sin_lo, sin_hi)


def _slot_major(n):
    j, i, d = jnp.meshgrid(jnp.arange(GROUP), jnp.arange(N_KV_HEADS), jnp.arange(HEAD_DIM), indexing="ij")
    return ((GROUP * i + j) * HEAD_DIM + d).reshape(n)


def kernel(x_prompt, x_sample, cache_k, cache_v, state_conv, norm_mix, w_in, sinks, w_o_attn,
           w_dw, b_dw, ln_conv_g, ln_conv_b, w_pw2, w_out, norm_mlp, w_up, w_down, norm_final):
    batch, seq, _ = x_prompt.shape
    streams, steps, _ = x_sample.shape
    aw = N_Q_HEADS * HEAD_DIM
    perm = _slot_major(aw)
    tabs_p = _rope_tables(jnp.arange(seq, dtype=jnp.int32))
    tabs_s = tuple(jnp.tile(tb, (streams, 1))
                   for tb in _rope_tables(PAST_LEN + jnp.arange(steps, dtype=jnp.int32)))
    row = lambda p: p.reshape(1, -1)

    xp = x_prompt
    xs = x_sample.reshape(streams * steps, D_MODEL)
    outs = [[] for _ in range(6)]
    for l in range(DEPTH):
        wl = w_in[l].astype(BF16)
        c0, c1, c2 = aw, aw + KV_WIDTH, aw + 2 * KV_WIDTH
        weights = (row(norm_mix[l]), wl[:, :c0][:, perm], wl[:, c0:c1], wl[:, c1:c2],
                   wl[:, c2:c2 + 2 * D_MODEL], wl[:, c2 + 2 * D_MODEL:],
                   w_o_attn[l][perm, :].astype(BF16), w_dw[l], row(b_dw[l]), row(ln_conv_g[l]),
                   row(ln_conv_b[l]), w_pw2[l].astype(BF16), w_out[l].astype(BF16))
        ffn_w = (row(norm_mlp[l]), w_up[l].astype(BF16), w_down[l].astype(BF16), row(norm_final))
        final = l == DEPTH - 1

        xp, kp, vp, cp = _mixer_prompt(xp, tabs_p, sinks[l], weights)
        xp = _ffn(xp.reshape(batch * seq, D_MODEL), *ffn_w, final, "ffn_prompt").reshape(batch, seq, D_MODEL)

        cpast = jnp.pad(state_conv[l], ((0, 0), (CONV_PAD, 0), (0, 0)))
        xs, kn, vn, cn = _mixer_sample(
            xs.reshape(streams, steps, D_MODEL), tabs_s,
            cache_k[l].reshape(streams, WINDOW, KV_WIDTH), cache_v[l].reshape(streams, WINDOW, KV_WIDTH),
            cpast, sinks[l], weights)
        xs = _ffn(xs, *ffn_w, final, "ffn_sample")

        heads = lambda z, n: z.reshape(-1, n, N_KV_HEADS, HEAD_DIM)
        for dst, val in zip(outs, (heads(kp, WINDOW), heads(vp, WINDOW), cp,
                                   heads(kn, steps), heads(vn, steps), cn)):
            dst.append(val)

    return (xp, xs.reshape(streams, steps, D_MODEL)) + tuple(jnp.stack(o) for o in outs)
```

```python
import functools

import jax
import jax.numpy as jnp
from jax import lax
from jax.experimental import pallas as pl
from jax.experimental.pallas import tpu as pltpu

D_MODEL = 1024
DEPTH = 2
PAST_LEN = 2048
CHUNK = 64
HEAD_DIM = 64
N_Q_HEADS = 16
N_KV_HEADS = 4
GROUP = N_Q_HEADS // N_KV_HEADS
KV_WIDTH = N_KV_HEADS * HEAD_DIM
WINDOW = 128
CONV_W = 31
CONV_KEEP = CONV_W - 1
D_FF = 4 * D_MODEL
ROPE_THETA = 10000.0
EPS = 1e-6
NEG_INF = -1e30

LANES = 128
MXU_WIDTH = 256
CONV_HIST = 32
CONV_PAD = CONV_HIST - CONV_KEEP
VMEM_LIMIT = 56 * 1024 * 1024

PROMPT_TILE = 256
FFN_TILE = 512

F32 = jnp.float32
BF16 = jnp.bfloat16


def _rms(x, g):
    return x * lax.rsqrt(jnp.mean(x * x, axis=-1, keepdims=True) + EPS) * g


def _dot(a, b):
    return jnp.dot(a, b, preferred_element_type=F32)


def _rope(x, cos, sin_lo, sin_hi):
    outs = []
    for b in range(x.shape[1] // LANES):
        xb = x[:, b * LANES:(b + 1) * LANES]
        up = pltpu.roll(xb, LANES - HEAD_DIM // 2, 1)
        dn = pltpu.roll(xb, HEAD_DIM // 2, 1)
        outs.append(xb * cos + up * sin_lo + dn * sin_hi)
    return jnp.concatenate(outs, axis=1)


def _attend(q, k, v, sink_ref, valid):
    nq = q.shape[0]
    lane_head = lax.broadcasted_iota(jnp.int32, (nq, KV_WIDTH), 1) // HEAD_DIM
    out = [None] * GROUP
    for i in range(N_KV_HEADS):
        sel = lane_head == i
        qm = jnp.concatenate(
            [jnp.where(sel, q[:, j * KV_WIDTH:(j + 1) * KV_WIDTH], jnp.zeros((), q.dtype))
             for j in range(GROUP)], axis=0)
        s = lax.dot_general(qm, k, (((1,), (1,)), ((), ())), preferred_element_type=F32)
        if valid is not None:
            s = jnp.where(valid, s, NEG_INF)
        ps = []
        for j in range(GROUP):
            sj = s[j * nq:(j + 1) * nq]
            sink = sink_ref[GROUP * i + j]
            m = jnp.maximum(jnp.max(sj, axis=-1, keepdims=True), sink)
            p = jnp.exp(sj - m)
            den = jnp.sum(p, axis=-1, keepdims=True) + jnp.exp(sink - m)
            ps.append((p * (1.0 / den)).astype(BF16))
        o = _dot(jnp.concatenate(ps, axis=0), v)
        for j in range(GROUP):
            oj = o[j * nq:(j + 1) * nq]
            out[j] = oj if i == 0 else jnp.where(sel, oj, out[j])
    return jnp.concatenate(out, axis=1)


def _dwconv(full_ref, r, wdw_ref, y_ref, yrow, row_block, lane0=0, lane1=D_MODEL):
    win = row_block + CONV_HIST
    for c in range(lane0, lane1, LANES):
        window = full_ref[r:r + win, c:c + LANES]
        acc = None
        for rho in range(8):
            shifted = window if rho == 0 else pltpu.roll(window, win - rho, 0)
            for a in range(CONV_HIST // 8 + 1):
                j = 8 * a + rho - CONV_PAD
                if 0 <= j < CONV_W:
                    term = shifted[8 * a:8 * a + row_block] * wdw_ref[j:j + 1, c:c + LANES]
                    acc = term if acc is None else acc + term
        y_ref[yrow:yrow + row_block, c:c + LANES] = acc


def _conv_tail(y, bdw, lng, lnb):
    y = y + bdw
    mu = jnp.mean(y, axis=-1, keepdims=True)
    yc = y - mu
    var = jnp.mean(yc * yc, axis=-1, keepdims=True)
    z = yc * lax.rsqrt(var + EPS) * lng + lnb
    return z * jax.nn.sigmoid(z)


def _merge_out(x, h, attn_bf16, conv_act_bf16, wg_ref, wo_ref, wpw2_ref, wout_ref):
    g = jax.nn.sigmoid(_dot(h, wg_ref[...]))
    merged = (g[:, :D_MODEL] * _dot(attn_bf16, wo_ref[...])
              + g[:, D_MODEL:] * _dot(conv_act_bf16, wpw2_ref[...]))
    return x + _dot(merged.astype(BF16), wout_ref[...])


def _mixer_prompt_kernel(sink_ref, x_ref, cq_ref, slq_ref, shq_ref, ck_ref, slk_ref, shk_ref,
                         nmix_ref, wq_ref, wk_ref, wv_ref, wu_ref, wg_ref, wo_ref, wdw_ref,
                         bdw_ref, lng_ref, lnb_ref, wpw2_ref, wout_ref,
                         xo_ref, kkeep_ref, vkeep_ref, ckeep_ref,
                         kh_ref, vh_ref, afull_ref, q_ref, at_ref, y_ref, *, tile):
    t = pl.program_id(1)

    @pl.when(t == 0)
    def _():
        kh_ref[0:WINDOW, :] = jnp.zeros((WINDOW, KV_WIDTH), BF16)
        vh_ref[0:WINDOW, :] = jnp.zeros((WINDOW, KV_WIDTH), BF16)
        afull_ref[0:CONV_HIST, :] = jnp.zeros((CONV_HIST, D_MODEL), F32)

    @pl.when(t > 0)
    def _():
        kh_ref[0:WINDOW, :] = kh_ref[tile:tile + WINDOW, :]
        vh_ref[0:WINDOW, :] = vh_ref[tile:tile + WINDOW, :]
        afull_ref[0:CONV_HIST, :] = afull_ref[tile:tile + CONV_HIST, :]

    x = x_ref[0]
    h = _rms(x, nmix_ref[...]).astype(BF16)

    for c0 in range(0, D_MODEL, MXU_WIDTH):
        c1 = c0 + MXU_WIDTH
        lin = _dot(h, wu_ref[:, c0:c1])
        gate = _dot(h, wu_ref[:, D_MODEL + c0:D_MODEL + c1])
        afull_ref[CONV_HIST:CONV_HIST + tile, c0:c1] = lin * jax.nn.sigmoid(gate)
        for r in range(0, tile, CHUNK):
            _dwconv(afull_ref, r, wdw_ref, y_ref, r, CHUNK, c0, c1)
    ckeep_ref[0] = afull_ref[CONV_HIST + tile - CONV_KEEP:CONV_HIST + tile, :]

    q_ref[...] = _rope(_dot(h, wq_ref[...]), cq_ref[...], slq_ref[...], shq_ref[...]).astype(BF16)
    k = _rope(_dot(h, wk_ref[...]), ck_ref[...], slk_ref[...], shk_ref[...])
    v = _dot(h, wv_ref[...])
    kh_ref[WINDOW:WINDOW + tile, :] = k.astype(BF16)
    vh_ref[WINDOW:WINDOW + tile, :] = v.astype(BF16)
    kkeep_ref[0] = k[tile - WINDOW:, :]
    vkeep_ref[0] = v[tile - WINDOW:, :]

    g = jax.nn.sigmoid(_dot(h, wg_ref[...]))
    conv_act = _conv_tail(y_ref[...], bdw_ref[...], lng_ref[...], lnb_ref[...]).astype(BF16)

    band = WINDOW + CHUNK
    key_chunk = lax.broadcasted_iota(jnp.int32, (1, band), 1) // CHUNK
    for c in range(tile // CHUNK):
        r0 = c * CHUNK
        valid = (t * (tile // CHUNK) + c - WINDOW // CHUNK + key_chunk) >= 0
        o = _attend(q_ref[r0:r0 + CHUNK, :], kh_ref[r0:r0 + band, :], vh_ref[r0:r0 + band, :],
                    sink_ref, valid)
        at_ref[r0:r0 + CHUNK, :] = o.astype(BF16)

    merged = (g[:, :D_MODEL] * _dot(at_ref[...], wo_ref[...])
              + g[:, D_MODEL:] * _dot(conv_act, wpw2_ref[...]))
    xo_ref[0] = x + _dot(merged.astype(BF16), wout_ref[...])


def _mixer_sample_kernel(sink_ref, x_ref, cq_ref, slq_ref, shq_ref, ck_ref, slk_ref, shk_ref,
                         kpast_ref, vpast_ref, cpast_ref,
                         nmix_ref, wq_ref, wk_ref, wv_ref, wu_ref, wg_ref, wo_ref, wdw_ref,
                         bdw_ref, lng_ref, lnb_ref, wpw2_ref, wout_ref,
                         xo_ref, knew_ref, vnew_ref, ckeep_ref,
                         kf_ref, vf_ref, afull_ref, at_ref, y_ref, *, streams, steps):
    x = x_ref[...]
    h = _rms(x, nmix_ref[...]).astype(BF16)
    q = _rope(_dot(h, wq_ref[...]), cq_ref[...], slq_ref[...], shq_ref[...]).astype(BF16)
    k = _rope(_dot(h, wk_ref[...]), ck_ref[...], slk_ref[...], shk_ref[...])
    v = _dot(h, wv_ref[...])
    knew_ref[...] = k
    vnew_ref[...] = v
    u = _dot(h, wu_ref[...])
    a = u[:, :D_MODEL] * jax.nn.sigmoid(u[:, D_MODEL:])

    for s in range(streams):
        rows = slice(s * steps, (s + 1) * steps)
        kf_ref[0:WINDOW, :] = kpast_ref[s].astype(BF16)
        vf_ref[0:WINDOW, :] = vpast_ref[s].astype(BF16)
        kf_ref[WINDOW:WINDOW + steps, :] = k[rows].astype(BF16)
        vf_ref[WINDOW:WINDOW + steps, :] = v[rows].astype(BF16)
        at_ref[rows, :] = _attend(q[rows], kf_ref[...], vf_ref[...], sink_ref, None).astype(BF16)

        afull_ref[0:CONV_HIST, :] = cpast_ref[s]
        afull_ref[CONV_HIST:CONV_HIST + steps, :] = a[rows]
        ckeep_ref[s] = afull_ref[CONV_HIST + steps - CONV_KEEP:CONV_HIST + steps, :]
        _dwconv(afull_ref, 0, wdw_ref, y_ref, s * steps, steps)

    conv_act = _conv_tail(y_ref[...], bdw_ref[...], lng_ref[...], lnb_ref[...]).astype(BF16)
    xo_ref[...] = _merge_out(x, h, at_ref[...], conv_act, wg_ref, wo_ref, wpw2_ref, wout_ref)


def _ffn_kernel(x_ref, nmlp_ref, wup_ref, wdown_ref, nfin_ref, o_ref, *, final):
    x = x_ref[...]
    hm = _rms(x, nmlp_ref[...]).astype(BF16)
    acc = x
    for c in range(0, D_FF, D_MODEL):
        r = jnp.maximum(_dot(hm, wup_ref[:, c:c + D_MODEL]), 0.0)
        acc = acc + _dot((r * r).astype(BF16), wdown_ref[c:c + D_MODEL, :])
    o_ref[...] = _rms(acc, nfin_ref[...]) if final else acc


def _const_spec(shape):
    nd = len(shape)
    return pl.BlockSpec(shape, lambda *_: (0,) * nd, pipeline_mode=pl.Buffered(1))


_SMEM_SPEC = pl.BlockSpec(memory_space=pltpu.SMEM)


def _mixer_weight_specs():
    return [
        _const_spec((1, D_MODEL)),
        _const_spec((D_MODEL, D_MODEL)),
        _const_spec((D_MODEL, KV_WIDTH)),
        _const_spec((D_MODEL, KV_WIDTH)),
        _const_spec((D_MODEL, 2 * D_MODEL)),
        _const_spec((D_MODEL, 2 * D_MODEL)),
        _const_spec((D_MODEL, D_MODEL)),
        _const_spec((CONV_W, D_MODEL)),
        _const_spec((1, D_MODEL)),
        _const_spec((1, D_MODEL)),
        _const_spec((1, D_MODEL)),
        _const_spec((D_MODEL, D_MODEL)),
        _const_spec((D_MODEL, D_MODEL)),
    ]


def _mixer_prompt(x, tabs, sink, weights):
    batch, seq, _ = x.shape
    tile = PROMPT_TILE
    row_spec = pl.BlockSpec((1, tile, D_MODEL), lambda b, t: (b, t, 0))
    tab_spec = pl.BlockSpec((tile, LANES), lambda b, t: (t, 0))
    keep_spec = lambda rows, width: pl.BlockSpec((1, rows, width), lambda b, t: (b, 0, 0))
    return pl.pallas_call(
        functools.partial(_mixer_prompt_kernel, tile=tile),
        grid=(batch, seq // tile),
        in_specs=[_SMEM_SPEC, row_spec] + [tab_spec] * 6 + _mixer_weight_specs(),
        out_specs=[row_spec, keep_spec(WINDOW, KV_WIDTH), keep_spec(WINDOW, KV_WIDTH),
                   keep_spec(CONV_KEEP, D_MODEL)],
        out_shape=[jax.ShapeDtypeStruct((batch, seq, D_MODEL), F32),
                   jax.ShapeDtypeStruct((batch, WINDOW, KV_WIDTH), F32),
                   jax.ShapeDtypeStruct((batch, WINDOW, KV_WIDTH), F32),
                   jax.ShapeDtypeStruct((batch, CONV_KEEP, D_MODEL), F32)],
        scratch_shapes=[pltpu.VMEM((WINDOW + tile, KV_WIDTH), BF16),
                        pltpu.VMEM((WINDOW + tile, KV_WIDTH), BF16),
                        pltpu.VMEM((CONV_HIST + tile, D_MODEL), F32),
                        pltpu.VMEM((tile, D_MODEL), BF16),
                        pltpu.VMEM((tile, D_MODEL), BF16),
                        pltpu.VMEM((tile, D_MODEL), F32)],
        compiler_params=pltpu.CompilerParams(
            dimension_semantics=("arbitrary", "arbitrary"), vmem_limit_bytes=VMEM_LIMIT),
        name="mixer_prompt",
    )(sink, x, *tabs, *weights)


def _mixer_sample(x, tabs, kpast, vpast, cpast, sink, weights):
    streams, steps, _ = x.shape
    rows = streams * steps
    full = lambda shape: pl.BlockSpec(shape, lambda i: (0,) * len(shape))
    return pl.pallas_call(
        functools.partial(_mixer_sample_kernel, streams=streams, steps=steps),
        grid=(1,),
        in_specs=[_SMEM_SPEC, full((rows, D_MODEL))] + [full((rows, LANES))] * 6
        + [full((streams, WINDOW, KV_WIDTH)), full((streams, WINDOW, KV_WIDTH)),
           full((streams, CONV_HIST, D_MODEL))] + _mixer_weight_specs(),
        out_specs=[full((rows, D_MODEL)), full((rows, KV_WIDTH)), full((rows, KV_WIDTH)),
                   full((streams, CONV_KEEP, D_MODEL))],
        out_shape=[jax.ShapeDtypeStruct((rows, D_MODEL), F32),
                   jax.ShapeDtypeStruct((rows, KV_WIDTH), F32),
                   jax.ShapeDtypeStruct((rows, KV_WIDTH), F32),
                   jax.ShapeDtypeStruct((streams, CONV_KEEP, D_MODEL), F32)],
        scratch_shapes=[pltpu.VMEM((WINDOW + steps, KV_WIDTH), BF16),
                        pltpu.VMEM((WINDOW + steps, KV_WIDTH), BF16),
                        pltpu.VMEM((CONV_HIST + steps, D_MODEL), F32),
                        pltpu.VMEM((rows, D_MODEL), BF16),
                        pltpu.VMEM((rows, D_MODEL), F32)],
        compiler_params=pltpu.CompilerParams(
            dimension_semantics=("arbitrary",), vmem_limit_bytes=VMEM_LIMIT),
        name="mixer_sample",
    )(sink, x.reshape(rows, D_MODEL), *tabs, kpast, vpast, cpast, *weights)


def _ffn(x, nmlp, wup, wdown, nfin, final, name):
    rows = x.shape[0]
    tile = min(FFN_TILE, rows)
    row_spec = pl.BlockSpec((tile, D_MODEL), lambda i: (i, 0))
    return pl.pallas_call(
        functools.partial(_ffn_kernel, final=final),
        grid=(rows // tile,),
        in_specs=[row_spec, _const_spec((1, D_MODEL)), _const_spec((D_MODEL, D_FF)),
                  _const_spec((D_FF, D_MODEL)), _const_spec((1, D_MODEL))],
        out_specs=row_spec,
        out_shape=jax.ShapeDtypeStruct((rows, D_MODEL), F32),
        compiler_params=pltpu.CompilerParams(
            dimension_semantics=("arbitrary",), vmem_limit_bytes=VMEM_LIMIT),
        name=name,
    )(x, nmlp, wup, wdown, nfin)


def _rope_tables(pos):
    half = HEAD_DIM // 2
    inv_freq = ROPE_THETA ** (-jnp.arange(half, dtype=F32) / half)
    ang = pos.astype(F32)[:, None] * inv_freq[None, :]
    cos, sin = jnp.cos(ang), jnp.sin(ang)
    zero = jnp.zeros_like(sin)
    reps = LANES // HEAD_DIM
    cos_t = jnp.tile(jnp.concatenate([cos, cos], axis=1), (1, reps))
    sin_lo = jnp.tile(jnp.concatenate([-sin, zero], axis=1), (1, reps))
    sin_hi = jnp.tile(jnp.concatenate([zero, sin], axis=1), (1, reps))
    scale = HEAD_DIM ** -0.5
    return (cos_t * scale, sin_lo * scale, sin_hi * scale, cos_t, sin_lo, sin_hi)


def _slot_major(n):
    j, i, d = jnp.meshgrid(jnp.arange(GROUP), jnp.arange(N_KV_HEADS), jnp.arange(HEAD_DIM), indexing="ij")
    return ((GROUP * i + j) * HEAD_DIM + d).reshape(n)


def kernel(x_prompt, x_sample, cache_k, cache_v, state_conv, norm_mix, w_in, sinks, w_o_attn,
           w_dw, b_dw, ln_conv_g, ln_conv_b, w_pw2, w_out, norm_mlp, w_up, w_down, norm_final):
    batch, seq, _ = x_prompt.shape
    streams, steps, _ = x_sample.shape
    aw = N_Q_HEADS * HEAD_DIM
    perm = _slot_major(aw)
    tabs_p = _rope_tables(jnp.arange(seq, dtype=jnp.int32))
    tabs_s = tuple(jnp.tile(tb, (streams, 1))
                   for tb in _rope_tables(PAST_LEN + jnp.arange(steps, dtype=jnp.int32)))
    row = lambda p: p.reshape(1, -1)

    xp = x_prompt
    xs = x_sample.reshape(streams * steps, D_MODEL)
    outs = [[] for _ in range(6)]
    for l in range(DEPTH):
        wl = w_in[l].astype(BF16)
        c0, c1, c2 = aw, aw + KV_WIDTH, aw + 2 * KV_WIDTH
        weights = (row(norm_mix[l]), wl[:, :c0][:, perm], wl[:, c0:c1], wl[:, c1:c2],
                   wl[:, c2:c2 + 2 * D_MODEL], wl[:, c2 + 2 * D_MODEL:],
                   w_o_attn[l][perm, :].astype(BF16), w_dw[l], row(b_dw[l]), row(ln_conv_g[l]),
                   row(ln_conv_b[l]), w_pw2[l].astype(BF16), w_out[l].astype(BF16))
        ffn_w = (row(norm_mlp[l]), w_up[l].astype(BF16), w_down[l].astype(BF16), row(norm_final))
        final = l == DEPTH - 1

        xp, kp, vp, cp = _mixer_prompt(xp, tabs_p, sinks[l], weights)
        xp = _ffn(xp.reshape(batch * seq, D_MODEL), *ffn_w, final, "ffn_prompt").reshape(batch, seq, D_MODEL)

        cpast = jnp.pad(state_conv[l], ((0, 0), (CONV_PAD, 0), (0, 0)))
        xs, kn, vn, cn = _mixer_sample(
            xs.reshape(streams, steps, D_MODEL), tabs_s,
            cache_k[l].reshape(streams, WINDOW, KV_WIDTH), cache_v[l].reshape(streams, WINDOW, KV_WIDTH),
            cpast, sinks[l], weights)
        xs = _ffn(xs, *ffn_w, final, "ffn_sample")

        heads = lambda z, n: z.reshape(-1, n, N_KV_HEADS, HEAD_DIM)
        for dst, val in zip(outs, (heads(kp, WINDOW), heads(vp, WINDOW), cp,
                                   heads(kn, steps), heads(vn, steps), cn)):
            dst.append(val)

    return (xp, xs.reshape(streams, steps, D_MODEL)) + tuple(jnp.stack(o) for o in outs)
```

```python
import functools

import jax
import jax.numpy as jnp
from jax import lax
from jax.experimental import pallas as pl
from jax.experimental.pallas import tpu as pltpu

D_MODEL = 1024
DEPTH = 2
PAST_LEN = 2048
CHUNK = 64
HEAD_DIM = 64
N_Q_HEADS = 16
N_KV_HEADS = 4
GROUP = N_Q_HEADS // N_KV_HEADS
KV_WIDTH = N_KV_HEADS * HEAD_DIM
WINDOW = 128
CONV_W = 31
CONV_KEEP = CONV_W - 1
D_FF = 4 * D_MODEL
ROPE_THETA = 10000.0
EPS = 1e-6
NEG_INF = -1e30

LANES = 128
MXU_WIDTH = 256
CONV_HIST = 32
CONV_PAD = CONV_HIST - CONV_KEEP
VMEM_LIMIT = 56 * 1024 * 1024

PROMPT_TILE = 256
PROMPT_SUB = 256
FFN_TILE = 512

F32 = jnp.float32
BF16 = jnp.bfloat16


def _rms(x, g):
    return x * lax.rsqrt(jnp.mean(x * x, axis=-1, keepdims=True) + EPS) * g


def _dot(a, b):
    return jnp.dot(a, b, preferred_element_type=F32)


def _rope(x, cos, sin_lo, sin_hi):
    outs = []
    for b in range(x.shape[1] // LANES):
        xb = x[:, b * LANES:(b + 1) * LANES]
        up = pltpu.roll(xb, LANES - HEAD_DIM // 2, 1)
        dn = pltpu.roll(xb, HEAD_DIM // 2, 1)
        outs.append(xb * cos + up * sin_lo + dn * sin_hi)
    return jnp.concatenate(outs, axis=1)


def _attend(q, k, v, sink_ref, valid):
    nq = q.shape[0]
    lane_head = lax.broadcasted_iota(jnp.int32, (nq, KV_WIDTH), 1) // HEAD_DIM
    out = [None] * GROUP
    for i in range(N_KV_HEADS):
        sel = lane_head == i
        qm = jnp.concatenate(
            [jnp.where(sel, q[:, j * KV_WIDTH:(j + 1) * KV_WIDTH], jnp.zeros((), q.dtype))
             for j in range(GROUP)], axis=0)
        s = lax.dot_general(qm, k, (((1,), (1,)), ((), ())), preferred_element_type=F32)
        if valid is not None:
            s = jnp.where(valid, s, NEG_INF)
        ps = []
        for j in range(GROUP):
            sj = s[j * nq:(j + 1) * nq]
            sink = sink_ref[GROUP * i + j]
            m = jnp.maximum(jnp.max(sj, axis=-1, keepdims=True), sink)
            p = jnp.exp(sj - m)
            den = jnp.sum(p, axis=-1, keepdims=True) + jnp.exp(sink - m)
            ps.append((p * (1.0 / den)).astype(BF16))
        o = _dot(jnp.concatenate(ps, axis=0), v)
        for j in range(GROUP):
            oj = o[j * nq:(j + 1) * nq]
            out[j] = oj if i == 0 else jnp.where(sel, oj, out[j])
    return jnp.concatenate(out, axis=1)


def _dwconv(full_ref, r, wdw_ref, y_ref, yrow, row_block, lane0=0, lane1=D_MODEL):
    win = row_block + CONV_HIST
    for c in range(lane0, lane1, LANES):
        window = full_ref[r:r + win, c:c + LANES]
        acc = None
        for rho in range(8):
            shifted = window if rho == 0 else pltpu.roll(window, win - rho, 0)
            for a in range(CONV_HIST // 8 + 1):
                j = 8 * a + rho - CONV_PAD
                if 0 <= j < CONV_W:
                    term = shifted[8 * a:8 * a + row_block] * wdw_ref[j:j + 1, c:c + LANES]
                    acc = term if acc is None else acc + term
        y_ref[yrow:yrow + row_block, c:c + LANES] = acc


def _conv_tail(y, bdw, lng, lnb):
    y = y + bdw
    mu = jnp.mean(y, axis=-1, keepdims=True)
    yc = y - mu
    var = jnp.mean(yc * yc, axis=-1, keepdims=True)
    z = yc * lax.rsqrt(var + EPS) * lng + lnb
    return z * jax.nn.sigmoid(z)


def _merge_out(x, h, attn_bf16, conv_act_bf16, wg_ref, wo_ref, wpw2_ref, wout_ref):
    g = jax.nn.sigmoid(_dot(h, wg_ref[...]))
    merged = (g[:, :D_MODEL] * _dot(attn_bf16, wo_ref[...])
              + g[:, D_MODEL:] * _dot(conv_act_bf16, wpw2_ref[...]))
    return x + _dot(merged.astype(BF16), wout_ref[...])


def _layer_prompt_kernel(sink_ref, x_ref, cq_ref, slq_ref, shq_ref, ck_ref, slk_ref, shk_ref,
                         nmix_ref, wq_ref, wk_ref, wv_ref, wu_ref, wg_ref, wo_ref, wdw_ref,
                         bdw_ref, lng_ref, lnb_ref, wpw2_ref, wout_ref,
                         nmlp_ref, wup_ref, wdown_ref, nfin_ref,
                         xo_ref, kkeep_ref, vkeep_ref, ckeep_ref,
                         kh_ref, vh_ref, afull_ref, q_ref, at_ref, y_ref, g_ref, xnext_ref, xcur_ref,
                         *, tile, sub, tiles_per_seq, n_tiles, final):
    s = pl.program_id(0)
    t = jnp.minimum(s, n_tiles - 1) % tiles_per_seq

    @pl.when(s == 0)
    def _():
        xcur_ref[...] = jnp.zeros((tile, D_MODEL), F32)

    @pl.when(s > 0)
    def _():
        xcur_ref[...] = xnext_ref[...]

    @pl.when(t == 0)
    def _():
        kh_ref[0:WINDOW, :] = jnp.zeros((WINDOW, KV_WIDTH), BF16)
        vh_ref[0:WINDOW, :] = jnp.zeros((WINDOW, KV_WIDTH), BF16)
        afull_ref[0:CONV_HIST, :] = jnp.zeros((CONV_HIST, D_MODEL), F32)

    @pl.when(t > 0)
    def _():
        kh_ref[0:WINDOW, :] = kh_ref[tile:tile + WINDOW, :]
        vh_ref[0:WINDOW, :] = vh_ref[tile:tile + WINDOW, :]
        afull_ref[0:CONV_HIST, :] = afull_ref[tile:tile + CONV_HIST, :]

    band = WINDOW + CHUNK
    key_chunk = lax.broadcasted_iota(jnp.int32, (1, band), 1) // CHUNK

    def project(r0):
        rows = slice(r0, r0 + sub)
        h = _rms(x_ref[rows, :], nmix_ref[...]).astype(BF16)
        for c0 in range(0, D_MODEL, MXU_WIDTH):
            c1 = c0 + MXU_WIDTH
            lin = _dot(h, wu_ref[:, c0:c1])
            gate = _dot(h, wu_ref[:, D_MODEL + c0:D_MODEL + c1])
            afull_ref[CONV_HIST + r0:CONV_HIST + r0 + sub, c0:c1] = lin * jax.nn.sigmoid(gate)
            for r in range(r0, r0 + sub, CHUNK):
                _dwconv(afull_ref, r, wdw_ref, y_ref, r, CHUNK, c0, c1)
        q_ref[rows, :] = _rope(_dot(h, wq_ref[...]), cq_ref[rows, :], slq_ref[rows, :],
                               shq_ref[rows, :]).astype(BF16)
        k = _rope(_dot(h, wk_ref[...]), ck_ref[rows, :], slk_ref[rows, :], shk_ref[rows, :])
        v = _dot(h, wv_ref[...])
        kh_ref[WINDOW + r0:WINDOW + r0 + sub, :] = k.astype(BF16)
        vh_ref[WINDOW + r0:WINDOW + r0 + sub, :] = v.astype(BF16)
        if r0 + sub == tile:
            kkeep_ref[0] = k[sub - WINDOW:, :]
            vkeep_ref[0] = v[sub - WINDOW:, :]
            ckeep_ref[0] = afull_ref[CONV_HIST + tile - CONV_KEEP:CONV_HIST + tile, :]
        g_ref[rows, :] = jax.nn.sigmoid(_dot(h, wg_ref[...]))

    def finish(r0):
        rows = slice(r0, r0 + sub)
        conv_act = _conv_tail(y_ref[rows, :], bdw_ref[...], lng_ref[...], lnb_ref[...]).astype(BF16)
        for c in range(r0 // CHUNK, (r0 + sub) // CHUNK):
            valid = None
            if c < WINDOW // CHUNK:
                valid = (t * (tile // CHUNK) + c - WINDOW // CHUNK + key_chunk) >= 0
            o = _attend(q_ref[c * CHUNK:(c + 1) * CHUNK, :], kh_ref[c * CHUNK:c * CHUNK + band, :],
                        vh_ref[c * CHUNK:c * CHUNK + band, :], sink_ref, valid)
            at_ref[c * CHUNK:(c + 1) * CHUNK, :] = o.astype(BF16)
        merged = (g_ref[rows, :D_MODEL] * _dot(at_ref[rows, :], wo_ref[...])
                  + g_ref[rows, D_MODEL:] * _dot(conv_act, wpw2_ref[...]))
        xnext_ref[rows, :] = x_ref[rows, :] + _dot(merged.astype(BF16), wout_ref[...])

    n_sub = tile // sub
    project(0)
    xo_ref[...] = _ffn_tile(xcur_ref[...], nmlp_ref, wup_ref, wdown_ref, nfin_ref, final)
    for i in range(1, n_sub):
        project(i * sub)
        finish((i - 1) * sub)
    finish((n_sub - 1) * sub)


def _mixer_sample_kernel(sink_ref, x_ref, cq_ref, slq_ref, shq_ref, ck_ref, slk_ref, shk_ref,
                         kpast_ref, vpast_ref, cpast_ref,
                         nmix_ref, wq_ref, wk_ref, wv_ref, wu_ref, wg_ref, wo_ref, wdw_ref,
                         bdw_ref, lng_ref, lnb_ref, wpw2_ref, wout_ref,
                         xo_ref, knew_ref, vnew_ref, ckeep_ref,
                         kf_ref, vf_ref, afull_ref, at_ref, y_ref, *, streams, steps):
    x = x_ref[...]
    h = _rms(x, nmix_ref[...]).astype(BF16)
    q = _rope(_dot(h, wq_ref[...]), cq_ref[...], slq_ref[...], shq_ref[...]).astype(BF16)
    k = _rope(_dot(h, wk_ref[...]), ck_ref[...], slk_ref[...], shk_ref[...])
    v = _dot(h, wv_ref[...])
    knew_ref[...] = k
    vnew_ref[...] = v
    u = _dot(h, wu_ref[...])
    a = u[:, :D_MODEL] * jax.nn.sigmoid(u[:, D_MODEL:])

    for s in range(streams):
        rows = slice(s * steps, (s + 1) * steps)
        kf_ref[0:WINDOW, :] = kpast_ref[s].astype(BF16)
        vf_ref[0:WINDOW, :] = vpast_ref[s].astype(BF16)
        kf_ref[WINDOW:WINDOW + steps, :] = k[rows].astype(BF16)
        vf_ref[WINDOW:WINDOW + steps, :] = v[rows].astype(BF16)
        at_ref[rows, :] = _attend(q[rows], kf_ref[...], vf_ref[...], sink_ref, None).astype(BF16)

        afull_ref[0:CONV_HIST, :] = cpast_ref[s]
        afull_ref[CONV_HIST:CONV_HIST + steps, :] = a[rows]
        ckeep_ref[s] = afull_ref[CONV_HIST + steps - CONV_KEEP:CONV_HIST + steps, :]
        _dwconv(afull_ref, 0, wdw_ref, y_ref, s * steps, steps)

    conv_act = _conv_tail(y_ref[...], bdw_ref[...], lng_ref[...], lnb_ref[...]).astype(BF16)
    xo_ref[...] = _merge_out(x, h, at_ref[...], conv_act, wg_ref, wo_ref, wpw2_ref, wout_ref)


def _ffn_tile(x, nmlp_ref, wup_ref, wdown_ref, nfin_ref, final):
    hm = _rms(x, nmlp_ref[...]).astype(BF16)
    acc = x
    for c in range(0, D_FF, D_MODEL):
        r = jnp.maximum(_dot(hm, wup_ref[:, c:c + D_MODEL]), 0.0)
        acc = acc + _dot((r * r).astype(BF16), wdown_ref[c:c + D_MODEL, :])
    return _rms(acc, nfin_ref[...]) if final else acc


def _ffn_kernel(x_ref, nmlp_ref, wup_ref, wdown_ref, nfin_ref, o_ref, *, final):
    o_ref[...] = _ffn_tile(x_ref[...], nmlp_ref, wup_ref, wdown_ref, nfin_ref, final)


def _const_spec(shape):
    nd = len(shape)
    return pl.BlockSpec(shape, lambda *_: (0,) * nd, pipeline_mode=pl.Buffered(1))


_SMEM_SPEC = pl.BlockSpec(memory_space=pltpu.SMEM)


def _mixer_weight_specs():
    return [
        _const_spec((1, D_MODEL)),
        _const_spec((D_MODEL, D_MODEL)),
        _const_spec((D_MODEL, KV_WIDTH)),
        _const_spec((D_MODEL, KV_WIDTH)),
        _const_spec((D_MODEL, 2 * D_MODEL)),
        _const_spec((D_MODEL, 2 * D_MODEL)),
        _const_spec((D_MODEL, D_MODEL)),
        _const_spec((CONV_W, D_MODEL)),
        _const_spec((1, D_MODEL)),
        _const_spec((1, D_MODEL)),
        _const_spec((1, D_MODEL)),
        _const_spec((D_MODEL, D_MODEL)),
        _const_spec((D_MODEL, D_MODEL)),
    ]


def _layer_prompt(x, tabs, sink, weights, ffn_weights, final):
    batch, seq, _ = x.shape
    tile = PROMPT_TILE
    tiles_per_seq = seq // tile
    n_tiles = batch * tiles_per_seq
    mixer_tile = lambda s: jnp.minimum(s, n_tiles - 1)
    ffn_tile = lambda s: jnp.maximum(s - 1, 0)
    x_spec = pl.BlockSpec((tile, D_MODEL), lambda s: (mixer_tile(s), 0))
    out_spec = pl.BlockSpec((tile, D_MODEL), lambda s: (ffn_tile(s), 0))
    tab_spec = pl.BlockSpec((tile, LANES), lambda s: (mixer_tile(s) % tiles_per_seq, 0))
    keep_spec = lambda rows, width: pl.BlockSpec(
        (1, rows, width), lambda s: (mixer_tile(s) // tiles_per_seq, 0, 0))
    ffn_specs = [_const_spec((1, D_MODEL)), _const_spec((D_MODEL, D_FF)),
                 _const_spec((D_FF, D_MODEL)), _const_spec((1, D_MODEL))]
    y, kk, vk, ck = pl.pallas_call(
        functools.partial(_layer_prompt_kernel, tile=tile, sub=PROMPT_SUB,
                          tiles_per_seq=tiles_per_seq, n_tiles=n_tiles, final=final),
        grid=(n_tiles + 1,),
        in_specs=[_SMEM_SPEC, x_spec] + [tab_spec] * 6 + _mixer_weight_specs() + ffn_specs,
        out_specs=[out_spec, keep_spec(WINDOW, KV_WIDTH), keep_spec(WINDOW, KV_WIDTH),
                   keep_spec(CONV_KEEP, D_MODEL)],
        out_shape=[jax.ShapeDtypeStruct((batch * seq, D_MODEL), F32),
                   jax.ShapeDtypeStruct((batch, WINDOW, KV_WIDTH), F32),
                   jax.ShapeDtypeStruct((batch, WINDOW, KV_WIDTH), F32),
                   jax.ShapeDtypeStruct((batch, CONV_KEEP, D_MODEL), F32)],
        scratch_shapes=[pltpu.VMEM((WINDOW + tile, KV_WIDTH), BF16),
                        pltpu.VMEM((WINDOW + tile, KV_WIDTH), BF16),
                        pltpu.VMEM((CONV_HIST + tile, D_MODEL), F32),
                        pltpu.VMEM((tile, D_MODEL), BF16),
                        pltpu.VMEM((tile, D_MODEL), BF16),
                        pltpu.VMEM((tile, D_MODEL), F32),
                        pltpu.VMEM((tile, 2 * D_MODEL), F32),
                        pltpu.VMEM((tile, D_MODEL), F32),
                        pltpu.VMEM((tile, D_MODEL), F32)],
        compiler_params=pltpu.CompilerParams(
            dimension_semantics=("arbitrary",), vmem_limit_bytes=VMEM_LIMIT),
        name="layer_prompt",
    )(sink, x.reshape(batch * seq, D_MODEL), *tabs, *weights, *ffn_weights)
    return y.reshape(batch, seq, D_MODEL), kk, vk, ck


def _mixer_sample(x, tabs, kpast, vpast, cpast, sink, weights):
    streams, steps, _ = x.shape
    rows = streams * steps
    full = lambda shape: pl.BlockSpec(shape, lambda i: (0,) * len(shape))
    return pl.pallas_call(
        functools.partial(_mixer_sample_kernel, streams=streams, steps=steps),
        grid=(1,),
        in_specs=[_SMEM_SPEC, full((rows, D_MODEL))] + [full((rows, LANES))] * 6
        + [full((streams, WINDOW, KV_WIDTH)), full((streams, WINDOW, KV_WIDTH)),
           full((streams, CONV_HIST, D_MODEL))] + _mixer_weight_specs(),
        out_specs=[full((rows, D_MODEL)), full((rows, KV_WIDTH)), full((rows, KV_WIDTH)),
                   full((streams, CONV_KEEP, D_MODEL))],
        out_shape=[jax.ShapeDtypeStruct((rows, D_MODEL), F32),
                   jax.ShapeDtypeStruct((rows, KV_WIDTH), F32),
                   jax.ShapeDtypeStruct((rows, KV_WIDTH), F32),
                   jax.ShapeDtypeStruct((streams, CONV_KEEP, D_MODEL), F32)],
        scratch_shapes=[pltpu.VMEM((WINDOW + steps, KV_WIDTH), BF16),
                        pltpu.VMEM((WINDOW + steps, KV_WIDTH), BF16),
                        pltpu.VMEM((CONV_HIST + steps, D_MODEL), F32),
                        pltpu.VMEM((rows, D_MODEL), BF16),
                        pltpu.VMEM((rows, D_MODEL), F32)],
        compiler_params=pltpu.CompilerParams(
            dimension_semantics=("arbitrary",), vmem_limit_bytes=VMEM_LIMIT),
        name="mixer_sample",
    )(sink, x.reshape(rows, D_MODEL), *tabs, kpast, vpast, cpast, *weights)


def _ffn(x, nmlp, wup, wdown, nfin, final, name):
    rows = x.shape[0]
    tile = min(FFN_TILE, rows)
    row_spec = pl.BlockSpec((tile, D_MODEL), lambda i: (i, 0))
    return pl.pallas_call(
        functools.partial(_ffn_kernel, final=final),
        grid=(rows // tile,),
        in_specs=[row_spec, _const_spec((1, D_MODEL)), _const_spec((D_MODEL, D_FF)),
                  _const_spec((D_FF, D_MODEL)), _const_spec((1, D_MODEL))],
        out_specs=row_spec,
        out_shape=jax.ShapeDtypeStruct((rows, D_MODEL), F32),
        compiler_params=pltpu.CompilerParams(
            dimension_semantics=("arbitrary",), vmem_limit_bytes=VMEM_LIMIT),
        name=name,
    )(x, nmlp, wup, wdown, nfin)


def _rope_tables(pos):
    half = HEAD_DIM // 2
    inv_freq = ROPE_THETA ** (-jnp.arange(half, dtype=F32) / half)
    ang = pos.astype(F32)[:, None] * inv_freq[None, :]
    cos, sin = jnp.cos(ang), jnp.sin(ang)
    zero = jnp.zeros_like(sin)
    reps = LANES // HEAD_DIM
    cos_t = jnp.tile(jnp.concatenate([cos, cos], axis=1), (1, reps))
    sin_lo = jnp.tile(jnp.concatenate([-sin, zero], axis=1), (1, reps))
    sin_hi = jnp.tile(jnp.concatenate([zero, sin], axis=1), (1, reps))
    scale = HEAD_DIM ** -0.5
    return (cos_t * scale, sin_lo * scale, sin_hi * scale, cos_t, sin_lo, sin_hi)


def _slot_major(n):
    j, i, d = jnp.meshgrid(jnp.arange(GROUP), jnp.arange(N_KV_HEADS), jnp.arange(HEAD_DIM), indexing="ij")
    return ((GROUP * i + j) * HEAD_DIM + d).reshape(n)


def kernel(x_prompt, x_sample, cache_k, cache_v, state_conv, norm_mix, w_in, sinks, w_o_attn,
           w_dw, b_dw, ln_conv_g, ln_conv_b, w_pw2, w_out, norm_mlp, w_up, w_down, norm_final):
    batch, seq, _ = x_prompt.shape
    streams, steps, _ = x_sample.shape
    aw = N_Q_HEADS * HEAD_DIM
    perm = _slot_major(aw)
    tabs_p = _rope_tables(jnp.arange(seq, dtype=jnp.int32))
    tabs_s = tuple(jnp.tile(tb, (streams, 1))
                   for tb in _rope_tables(PAST_LEN + jnp.arange(steps, dtype=jnp.int32)))
    row = lambda p: p.reshape(1, -1)

    xp = x_prompt
    xs = x_sample.reshape(streams * steps, D_MODEL)
    outs = [[] for _ in range(6)]
    for l in range(DEPTH):
        wl = w_in[l].astype(BF16)
        c0, c1, c2 = aw, aw + KV_WIDTH, aw + 2 * KV_WIDTH
        weights = (row(norm_mix[l]), wl[:, :c0][:, perm], wl[:, c0:c1], wl[:, c1:c2],
                   wl[:, c2:c2 + 2 * D_MODEL], wl[:, c2 + 2 * D_MODEL:],
                   w_o_attn[l][perm, :].astype(BF16), w_dw[l], row(b_dw[l]), row(ln_conv_g[l]),
                   row(ln_conv_b[l]), w_pw2[l].astype(BF16), w_out[l].astype(BF16))
        ffn_w = (row(norm_mlp[l]), w_up[l].astype(BF16), w_down[l].astype(BF16), row(norm_final))
        final = l == DEPTH - 1

        xp, kp, vp, cp = _layer_prompt(xp, tabs_p, sinks[l], weights, ffn_w, final)

        cpast = jnp.pad(state_conv[l], ((0, 0), (CONV_PAD, 0), (0, 0)))
        xs, kn, vn, cn = _mixer_sample(
            xs.reshape(streams, steps, D_MODEL), tabs_s,
            cache_k[l].reshape(streams, WINDOW, KV_WIDTH), cache_v[l].reshape(streams, WINDOW, KV_WIDTH),
            cpast, sinks[l], weights)
        xs = _ffn(xs, *ffn_w, final, "ffn_sample")

        heads = lambda z, n: z.reshape(-1, n, N_KV_HEADS, HEAD_DIM)
        for dst, val in zip(outs, (heads(kp, WINDOW), heads(vp, WINDOW), cp,
                                   heads(kn, steps), heads(vn, steps), cn)):
            dst.append(val)

    return (xp, xs.reshape(streams, steps, D_MODEL)) + tuple(jnp.stack(o) for o in outs)
```

```python
import functools

import jax
import jax.numpy as jnp
from jax import lax
from jax.experimental import pallas as pl
from jax.experimental.pallas import tpu as pltpu

D_MODEL = 1024
DEPTH = 2
PAST_LEN = 2048
CHUNK = 64
HEAD_DIM = 64
N_Q_HEADS = 16
N_KV_HEADS = 4
GROUP = N_Q_HEADS // N_KV_HEADS
KV_WIDTH = N_KV_HEADS * HEAD_DIM
WINDOW = 128
CONV_W = 31
CONV_KEEP = CONV_W - 1
D_FF = 4 * D_MODEL
REST_K = 0
REST_V = REST_K + KV_WIDTH
REST_U = REST_V + KV_WIDTH
REST_G = REST_U + 2 * D_MODEL
REST_WIDTH = REST_G + 2 * D_MODEL
ROPE_THETA = 10000.0
EPS = 1e-6
NEG_INF = -1e30

LANES = 128
MXU_WIDTH = 256
CONV_HIST = 32
CONV_PAD = CONV_HIST - CONV_KEEP
CONV_ROWS = 128
VMEM_LIMIT = 56 * 1024 * 1024

PROMPT_TILE = 256
FFN_TILE = 512

F32 = jnp.float32
BF16 = jnp.bfloat16


def _rms(x, g):
    return x * lax.rsqrt(jnp.mean(x * x, axis=-1, keepdims=True) + EPS) * g


def _dot(a, b):
    return jnp.dot(a, b, preferred_element_type=F32)


def _rope(x, cos, sin_lo, sin_hi):
    outs = []
    for b in range(x.shape[1] // LANES):
        xb = x[:, b * LANES:(b + 1) * LANES]
        up = pltpu.roll(xb, LANES - HEAD_DIM // 2, 1)
        dn = pltpu.roll(xb, HEAD_DIM // 2, 1)
        outs.append(xb * cos + up * sin_lo + dn * sin_hi)
    return jnp.concatenate(outs, axis=1)


def _attend(q, k, v, sink_ref, valid):
    nq = q.shape[0]
    lane_head = lax.broadcasted_iota(jnp.int32, (nq, KV_WIDTH), 1) // HEAD_DIM
    out = [None] * GROUP
    for i in range(N_KV_HEADS):
        sel = lane_head == i
        qm = jnp.concatenate(
            [jnp.where(sel, q[:, j * KV_WIDTH:(j + 1) * KV_WIDTH], jnp.zeros((), q.dtype))
             for j in range(GROUP)], axis=0)
        s = lax.dot_general(qm, k, (((1,), (1,)), ((), ())), preferred_element_type=F32)
        if valid is not None:
            s = jnp.where(valid, s, NEG_INF)
        ps = []
        for j in range(GROUP):
            sj = s[j * nq:(j + 1) * nq]
            sink = sink_ref[GROUP * i + j]
            m = jnp.maximum(jnp.max(sj, axis=-1, keepdims=True), sink)
            p = jnp.exp(sj - m)
            den = jnp.sum(p, axis=-1, keepdims=True) + jnp.exp(sink - m)
            ps.append((p * (1.0 / den)).astype(BF16))
        o = _dot(jnp.concatenate(ps, axis=0), v)
        for j in range(GROUP):
            oj = o[j * nq:(j + 1) * nq]
            out[j] = oj if i == 0 else jnp.where(sel, oj, out[j])
    return jnp.concatenate(out, axis=1)


def _dwconv(full_ref, r, wdw_ref, y_ref, yrow, row_block, lane0=0, lane1=D_MODEL):
    win = row_block + CONV_HIST
    for c in range(lane0, lane1, LANES):
        window = full_ref[r:r + win, c:c + LANES]
        acc = None
        for rho in range(8):
            shifted = window if rho == 0 else pltpu.roll(window, win - rho, 0)
            for a in range(CONV_HIST // 8 + 1):
                j = 8 * a + rho - CONV_PAD
                if 0 <= j < CONV_W:
                    term = shifted[8 * a:8 * a + row_block] * wdw_ref[j:j + 1, c:c + LANES]
                    acc = term if acc is None else acc + term
        y_ref[yrow:yrow + row_block, c:c + LANES] = acc


def _conv_tail(y, bdw, lng, lnb):
    y = y + bdw
    mu = jnp.mean(y, axis=-1, keepdims=True)
    yc = y - mu
    var = jnp.mean(yc * yc, axis=-1, keepdims=True)
    z = yc * lax.rsqrt(var + EPS) * lng + lnb
    return z * jax.nn.sigmoid(z)


def _merge_out(x, h, attn_bf16, conv_act_bf16, wr_ref, wo_ref, wpw2_ref, wout_ref):
    g = jax.nn.sigmoid(_dot(h, wr_ref[:, REST_G:]))
    merged = (g[:, :D_MODEL] * _dot(attn_bf16, wo_ref[...])
              + g[:, D_MODEL:] * _dot(conv_act_bf16, wpw2_ref[...]))
    return x + _dot(merged.astype(BF16), wout_ref[...])


def _layer_prompt_kernel(sink_ref, x_ref, cq_ref, slq_ref, shq_ref, ck_ref, slk_ref, shk_ref,
                         nmix_ref, wq_ref, wr_ref, wo_ref, wdw_ref,
                         bdw_ref, lng_ref, lnb_ref, wpw2_ref, wout_ref,
                         nmlp_ref, wup_ref, wdown_ref, nfin_ref,
                         xo_ref, kkeep_ref, vkeep_ref, ckeep_ref,
                         kh_ref, vh_ref, afull_ref, q_ref, at_ref, y_ref, g_ref, xnext_ref, xcur_ref,
                         *, tile, tiles_per_seq, n_tiles, final):
    s = pl.program_id(0)
    t = jnp.minimum(s, n_tiles - 1) % tiles_per_seq

    @pl.when(s == 0)
    def _():
        xcur_ref[...] = jnp.zeros((tile, D_MODEL), F32)

    @pl.when(s > 0)
    def _():
        xcur_ref[...] = xnext_ref[...]

    @pl.when(t == 0)
    def _():
        kh_ref[0:WINDOW, :] = jnp.zeros((WINDOW, KV_WIDTH), BF16)
        vh_ref[0:WINDOW, :] = jnp.zeros((WINDOW, KV_WIDTH), BF16)
        afull_ref[0:CONV_HIST, :] = jnp.zeros((CONV_HIST, D_MODEL), F32)

    @pl.when(t > 0)
    def _():
        kh_ref[0:WINDOW, :] = kh_ref[tile:tile + WINDOW, :]
        vh_ref[0:WINDOW, :] = vh_ref[tile:tile + WINDOW, :]
        afull_ref[0:CONV_HIST, :] = afull_ref[tile:tile + CONV_HIST, :]

    band = WINDOW + CHUNK
    key_chunk = lax.broadcasted_iota(jnp.int32, (1, band), 1) // CHUNK

    h = _rms(x_ref[...], nmix_ref[...]).astype(BF16)
    for c0 in range(0, D_MODEL, MXU_WIDTH):
        c1 = c0 + MXU_WIDTH
        lin = _dot(h, wr_ref[:, REST_U + c0:REST_U + c1])
        gate = _dot(h, wr_ref[:, REST_U + D_MODEL + c0:REST_U + D_MODEL + c1])
        afull_ref[CONV_HIST:CONV_HIST + tile, c0:c1] = lin * jax.nn.sigmoid(gate)
        for r in range(0, tile, CONV_ROWS):
            _dwconv(afull_ref, r, wdw_ref, y_ref, r, CONV_ROWS, c0, c1)
    q_ref[...] = _rope(_dot(h, wq_ref[...]), cq_ref[...], slq_ref[...], shq_ref[...]).astype(BF16)
    k = _rope(_dot(h, wr_ref[:, REST_K:REST_V]), ck_ref[...], slk_ref[...], shk_ref[...])
    v = _dot(h, wr_ref[:, REST_V:REST_U])
    kh_ref[WINDOW:WINDOW + tile, :] = k.astype(BF16)
    vh_ref[WINDOW:WINDOW + tile, :] = v.astype(BF16)
    kkeep_ref[0] = k[tile - WINDOW:, :]
    vkeep_ref[0] = v[tile - WINDOW:, :]
    ckeep_ref[0] = afull_ref[CONV_HIST + tile - CONV_KEEP:CONV_HIST + tile, :]
    g_ref[...] = jax.nn.sigmoid(_dot(h, wr_ref[:, REST_G:]))

    xo_ref[...] = _ffn_tile(xcur_ref[...], nmlp_ref, wup_ref, wdown_ref, nfin_ref, final)
    n_chunks = tile // CHUNK

    conv_act = _conv_tail(y_ref[...], bdw_ref[...], lng_ref[...], lnb_ref[...]).astype(BF16)
    for c in range(n_chunks):
        valid = None
        if c < WINDOW // CHUNK:
            valid = (t * n_chunks + c - WINDOW // CHUNK + key_chunk) >= 0
        o = _attend(q_ref[c * CHUNK:(c + 1) * CHUNK, :], kh_ref[c * CHUNK:c * CHUNK + band, :],
                    vh_ref[c * CHUNK:c * CHUNK + band, :], sink_ref, valid)
        at_ref[c * CHUNK:(c + 1) * CHUNK, :] = o.astype(BF16)

    merged = (g_ref[:, :D_MODEL] * _dot(at_ref[...], wo_ref[...])
              + g_ref[:, D_MODEL:] * _dot(conv_act, wpw2_ref[...]))
    xnext_ref[...] = x_ref[...] + _dot(merged.astype(BF16), wout_ref[...])


def _mixer_sample_kernel(sink_ref, x_ref, cq_ref, slq_ref, shq_ref, ck_ref, slk_ref, shk_ref,
                         kpast_ref, vpast_ref, cpast_ref,
                         nmix_ref, wq_ref, wr_ref, wo_ref, wdw_ref,
                         bdw_ref, lng_ref, lnb_ref, wpw2_ref, wout_ref,
                         xo_ref, knew_ref, vnew_ref, ckeep_ref,
                         kf_ref, vf_ref, afull_ref, at_ref, y_ref, *, streams, steps):
    x = x_ref[...]
    h = _rms(x, nmix_ref[...]).astype(BF16)
    q = _rope(_dot(h, wq_ref[...]), cq_ref[...], slq_ref[...], shq_ref[...]).astype(BF16)
    k = _rope(_dot(h, wr_ref[:, REST_K:REST_V]), ck_ref[...], slk_ref[...], shk_ref[...])
    v = _dot(h, wr_ref[:, REST_V:REST_U])
    knew_ref[...] = k
    vnew_ref[...] = v
    u = _dot(h, wr_ref[:, REST_U:REST_G])
    a = u[:, :D_MODEL] * jax.nn.sigmoid(u[:, D_MODEL:])

    for s in range(streams):
        rows = slice(s * steps, (s + 1) * steps)
        kf_ref[0:WINDOW, :] = kpast_ref[s].astype(BF16)
        vf_ref[0:WINDOW, :] = vpast_ref[s].astype(BF16)
        kf_ref[WINDOW:WINDOW + steps, :] = k[rows].astype(BF16)
        vf_ref[WINDOW:WINDOW + steps, :] = v[rows].astype(BF16)
        at_ref[rows, :] = _attend(q[rows], kf_ref[...], vf_ref[...], sink_ref, None).astype(BF16)

        afull_ref[0:CONV_HIST, :] = cpast_ref[s]
        afull_ref[CONV_HIST:CONV_HIST + steps, :] = a[rows]
        ckeep_ref[s] = afull_ref[CONV_HIST + steps - CONV_KEEP:CONV_HIST + steps, :]
        _dwconv(afull_ref, 0, wdw_ref, y_ref, s * steps, steps)

    conv_act = _conv_tail(y_ref[...], bdw_ref[...], lng_ref[...], lnb_ref[...]).astype(BF16)
    xo_ref[...] = _merge_out(x, h, at_ref[...], conv_act, wr_ref, wo_ref, wpw2_ref, wout_ref)


def _ffn_tile(x, nmlp_ref, wup_ref, wdown_ref, nfin_ref, final):
    hm = _rms(x, nmlp_ref[...]).astype(BF16)
    acc = x
    for c in range(0, D_FF, D_MODEL):
        r = jnp.maximum(_dot(hm, wup_ref[:, c:c + D_MODEL]), 0.0)
        acc = acc + _dot((r * r).astype(BF16), wdown_ref[c:c + D_MODEL, :])
    return _rms(acc, nfin_ref[...]) if final else acc


def _ffn_kernel(x_ref, nmlp_ref, wup_ref, wdown_ref, nfin_ref, o_ref, *, final):
    o_ref[...] = _ffn_tile(x_ref[...], nmlp_ref, wup_ref, wdown_ref, nfin_ref, final)


def _const_spec(shape):
    nd = len(shape)
    return pl.BlockSpec(shape, lambda *_: (0,) * nd, pipeline_mode=pl.Buffered(1))


_SMEM_SPEC = pl.BlockSpec(memory_space=pltpu.SMEM)


def _mixer_weight_specs():
    return [
        _const_spec((1, D_MODEL)),
        _const_spec((D_MODEL, D_MODEL)),
        _const_spec((D_MODEL, REST_WIDTH)),
        _const_spec((D_MODEL, D_MODEL)),
        _const_spec((CONV_W, D_MODEL)),
        _const_spec((1, D_MODEL)),
        _const_spec((1, D_MODEL)),
        _const_spec((1, D_MODEL)),
        _const_spec((D_MODEL, D_MODEL)),
        _const_spec((D_MODEL, D_MODEL)),
    ]


def _layer_prompt(x, tabs, sink, weights, ffn_weights, final):
    batch, seq, _ = x.shape
    tile = PROMPT_TILE
    tiles_per_seq = seq // tile
    n_tiles = batch * tiles_per_seq
    mixer_tile = lambda s: jnp.minimum(s, n_tiles - 1)
    ffn_tile = lambda s: jnp.maximum(s - 1, 0)
    x_spec = pl.BlockSpec((tile, D_MODEL), lambda s: (mixer_tile(s), 0))
    out_spec = pl.BlockSpec((tile, D_MODEL), lambda s: (ffn_tile(s), 0))
    tab_spec = pl.BlockSpec((tile, LANES), lambda s: (mixer_tile(s) % tiles_per_seq, 0))
    keep_spec = lambda rows, width: pl.BlockSpec(
        (1, rows, width), lambda s: (mixer_tile(s) // tiles_per_seq, 0, 0))
    ffn_specs = [_const_spec((1, D_MODEL)), _const_spec((D_MODEL, D_FF)),
                 _const_spec((D_FF, D_MODEL)), _const_spec((1, D_MODEL))]
    y, kk, vk, ck = pl.pallas_call(
        functools.partial(_layer_prompt_kernel, tile=tile,
                          tiles_per_seq=tiles_per_seq, n_tiles=n_tiles, final=final),
        grid=(n_tiles + 1,),
        in_specs=[_SMEM_SPEC, x_spec] + [tab_spec] * 6 + _mixer_weight_specs() + ffn_specs,
        out_specs=[out_spec, keep_spec(WINDOW, KV_WIDTH), keep_spec(WINDOW, KV_WIDTH),
                   keep_spec(CONV_KEEP, D_MODEL)],
        out_shape=[jax.ShapeDtypeStruct((batch * seq, D_MODEL), F32),
                   jax.ShapeDtypeStruct((batch, WINDOW, KV_WIDTH), F32),
                   jax.ShapeDtypeStruct((batch, WINDOW, KV_WIDTH), F32),
                   jax.ShapeDtypeStruct((batch, CONV_KEEP, D_MODEL), F32)],
        scratch_shapes=[pltpu.VMEM((WINDOW + tile, KV_WIDTH), BF16),
                        pltpu.VMEM((WINDOW + tile, KV_WIDTH), BF16),
                        pltpu.VMEM((CONV_HIST + tile, D_MODEL), F32),
                        pltpu.VMEM((tile, D_MODEL), BF16),
                        pltpu.VMEM((tile, D_MODEL), BF16),
                        pltpu.VMEM((tile, D_MODEL), F32),
                        pltpu.VMEM((tile, 2 * D_MODEL), F32),
                        pltpu.VMEM((tile, D_MODEL), F32),
                        pltpu.VMEM((tile, D_MODEL), F32)],
        compiler_params=pltpu.CompilerParams(
            dimension_semantics=("arbitrary",), vmem_limit_bytes=VMEM_LIMIT),
        name="layer_prompt",
    )(sink, x.reshape(batch * seq, D_MODEL), *tabs, *weights, *ffn_weights)
    return y.reshape(batch, seq, D_MODEL), kk, vk, ck


def _mixer_sample(x, tabs, kpast, vpast, cpast, sink, weights):
    streams, steps, _ = x.shape
    rows = streams * steps
    full = lambda shape: pl.BlockSpec(shape, lambda i: (0,) * len(shape))
    return pl.pallas_call(
        functools.partial(_mixer_sample_kernel, streams=streams, steps=steps),
        grid=(1,),
        in_specs=[_SMEM_SPEC, full((rows, D_MODEL))] + [full((rows, LANES))] * 6
        + [full((streams, WINDOW, KV_WIDTH)), full((streams, WINDOW, KV_WIDTH)),
           full((streams, CONV_HIST, D_MODEL))] + _mixer_weight_specs(),
        out_specs=[full((rows, D_MODEL)), full((rows, KV_WIDTH)), full((rows, KV_WIDTH)),
                   full((streams, CONV_KEEP, D_MODEL))],
        out_shape=[jax.ShapeDtypeStruct((rows, D_MODEL), F32),
                   jax.ShapeDtypeStruct((rows, KV_WIDTH), F32),
                   jax.ShapeDtypeStruct((rows, KV_WIDTH), F32),
                   jax.ShapeDtypeStruct((streams, CONV_KEEP, D_MODEL), F32)],
        scratch_shapes=[pltpu.VMEM((WINDOW + steps, KV_WIDTH), BF16),
                        pltpu.VMEM((WINDOW + steps, KV_WIDTH), BF16),
                        pltpu.VMEM((CONV_HIST + steps, D_MODEL), F32),
                        pltpu.VMEM((rows, D_MODEL), BF16),
                        pltpu.VMEM((rows, D_MODEL), F32)],
        compiler_params=pltpu.CompilerParams(
            dimension_semantics=("arbitrary",), vmem_limit_bytes=VMEM_LIMIT),
        name="mixer_sample",
    )(sink, x.reshape(rows, D_MODEL), *tabs, kpast, vpast, cpast, *weights)


def _ffn(x, nmlp, wup, wdown, nfin, final, name):
    rows = x.shape[0]
    tile = min(FFN_TILE, rows)
    row_spec = pl.BlockSpec((tile, D_MODEL), lambda i: (i, 0))
    return pl.pallas_call(
        functools.partial(_ffn_kernel, final=final),
        grid=(rows // tile,),
        in_specs=[row_spec, _const_spec((1, D_MODEL)), _const_spec((D_MODEL, D_FF)),
                  _const_spec((D_FF, D_MODEL)), _const_spec((1, D_MODEL))],
        out_specs=row_spec,
        out_shape=jax.ShapeDtypeStruct((rows, D_MODEL), F32),
        compiler_params=pltpu.CompilerParams(
            dimension_semantics=("arbitrary",), vmem_limit_bytes=VMEM_LIMIT),
        name=name,
    )(x, nmlp, wup, wdown, nfin)


def _rope_tables(pos):
    half = HEAD_DIM // 2
    inv_freq = ROPE_THETA ** (-jnp.arange(half, dtype=F32) / half)
    ang = pos.astype(F32)[:, None] * inv_freq[None, :]
    cos, sin = jnp.cos(ang), jnp.sin(ang)
    zero = jnp.zeros_like(sin)
    reps = LANES // HEAD_DIM
    cos_t = jnp.tile(jnp.concatenate([cos, cos], axis=1), (1, reps))
    sin_lo = jnp.tile(jnp.concatenate([-sin, zero], axis=1), (1, reps))
    sin_hi = jnp.tile(jnp.concatenate([zero, sin], axis=1), (1, reps))
    scale = HEAD_DIM ** -0.5
    return (cos_t * scale, sin_lo * scale, sin_hi * scale, cos_t, sin_lo, sin_hi)


def _slot_major(n):
    j, i, d = jnp.meshgrid(jnp.arange(GROUP), jnp.arange(N_KV_HEADS), jnp.arange(HEAD_DIM), indexing="ij")
    return ((GROUP * i + j) * HEAD_DIM + d).reshape(n)


def kernel(x_prompt, x_sample, cache_k, cache_v, state_conv, norm_mix, w_in, sinks, w_o_attn,
           w_dw, b_dw, ln_conv_g, ln_conv_b, w_pw2, w_out, norm_mlp, w_up, w_down, norm_final):
    batch, seq, _ = x_prompt.shape
    streams, steps, _ = x_sample.shape
    aw = N_Q_HEADS * HEAD_DIM
    perm = _slot_major(aw)
    tabs_p = _rope_tables(jnp.arange(seq, dtype=jnp.int32))
    tabs_s = tuple(jnp.tile(tb, (streams, 1))
                   for tb in _rope_tables(PAST_LEN + jnp.arange(steps, dtype=jnp.int32)))
    row = lambda p: p.reshape(1, -1)

    xp = x_prompt
    xs = x_sample.reshape(streams * steps, D_MODEL)
    outs = [[] for _ in range(6)]
    for l in range(DEPTH):
        weights = (row(norm_mix[l]), w_in[l][:, :aw][:, perm].astype(BF16), w_in[l][:, aw:].astype(BF16),
                   w_o_attn[l][perm, :].astype(BF16), w_dw[l], row(b_dw[l]), row(ln_conv_g[l]),
                   row(ln_conv_b[l]), w_pw2[l].astype(BF16), w_out[l].astype(BF16))
        ffn_w = (row(norm_mlp[l]), w_up[l].astype(BF16), w_down[l].astype(BF16), row(norm_final))
        final = l == DEPTH - 1

        xp, kp, vp, cp = _layer_prompt(xp, tabs_p, sinks[l], weights, ffn_w, final)

        cpast = jnp.pad(state_conv[l], ((0, 0), (CONV_PAD, 0), (0, 0)))
        xs, kn, vn, cn = _mixer_sample(
            xs.reshape(streams, steps, D_MODEL), tabs_s,
            cache_k[l].reshape(streams, WINDOW, KV_WIDTH), cache_v[l].reshape(streams, WINDOW, KV_WIDTH),
            cpast, sinks[l], weights)
        xs = _ffn(xs, *ffn_w, final, "ffn_sample")

        heads = lambda z, n: z.reshape(-1, n, N_KV_HEADS, HEAD_DIM)
        for dst, val in zip(outs, (heads(kp, WINDOW), heads(vp, WINDOW), cp,
                                   heads(kn, steps), heads(vn, steps), cn)):
            dst.append(val)

    return (xp, xs.reshape(streams, steps, D_MODEL)) + tuple(jnp.stack(o) for o in outs)
```

```python
import functools

import jax
import jax.numpy as jnp
from jax import lax
from jax.experimental import pallas as pl
from jax.experimental.pallas import tpu as pltpu

D_MODEL = 1024
DEPTH = 2
PAST_LEN = 2048
CHUNK = 64
HEAD_DIM = 64
N_Q_HEADS = 16
N_KV_HEADS = 4
GROUP = N_Q_HEADS // N_KV_HEADS
KV_WIDTH = N_KV_HEADS * HEAD_DIM
WINDOW = 128
CONV_W = 31
CONV_KEEP = CONV_W - 1
D_FF = 4 * D_MODEL
REST_K = 0
REST_V = REST_K + KV_WIDTH
REST_U = REST_V + KV_WIDTH
REST_G = REST_U + 2 * D_MODEL
REST_WIDTH = REST_G + 2 * D_MODEL
ROPE_THETA = 10000.0
EPS = 1e-6
NEG_INF = -1e30

LANES = 128
MXU_WIDTH = 256
CONV_HIST = 32
CONV_PAD = CONV_HIST - CONV_KEEP
CONV_ROWS = 128
VMEM_LIMIT = 56 * 1024 * 1024

PROMPT_TILE = 256
FFN_TILE = 512

F32 = jnp.float32
BF16 = jnp.bfloat16


def _rms(x, g):
    return x * lax.rsqrt(jnp.mean(x * x, axis=-1, keepdims=True) + EPS) * g


def _dot(a, b):
    return jnp.dot(a, b, preferred_element_type=F32)


def _rope(x, cos, sin_lo, sin_hi):
    outs = []
    for b in range(x.shape[1] // LANES):
        xb = x[:, b * LANES:(b + 1) * LANES]
        up = pltpu.roll(xb, LANES - HEAD_DIM // 2, 1)
        dn = pltpu.roll(xb, HEAD_DIM // 2, 1)
        outs.append(xb * cos + up * sin_lo + dn * sin_hi)
    return jnp.concatenate(outs, axis=1)


def _attend(q, k, v, sink_ref, valid):
    nq = q.shape[0]
    lane_head = lax.broadcasted_iota(jnp.int32, (nq, KV_WIDTH), 1) // HEAD_DIM
    out = [None] * GROUP
    for i in range(N_KV_HEADS):
        sel = lane_head == i
        qm = jnp.concatenate(
            [jnp.where(sel, q[:, j * KV_WIDTH:(j + 1) * KV_WIDTH], jnp.zeros((), q.dtype))
             for j in range(GROUP)], axis=0)
        s = lax.dot_general(qm, k, (((1,), (1,)), ((), ())), preferred_element_type=F32)
        if valid is not None:
            s = jnp.where(valid, s, NEG_INF)
        ps = []
        for j in range(GROUP):
            sj = s[j * nq:(j + 1) * nq]
            sink = sink_ref[GROUP * i + j]
            m = jnp.maximum(jnp.max(sj, axis=-1, keepdims=True), sink)
            p = jnp.exp(sj - m)
            den = jnp.sum(p, axis=-1, keepdims=True) + jnp.exp(sink - m)
            ps.append((p * (1.0 / den)).astype(BF16))
        o = _dot(jnp.concatenate(ps, axis=0), v)
        for j in range(GROUP):
            oj = o[j * nq:(j + 1) * nq]
            out[j] = oj if i == 0 else jnp.where(sel, oj, out[j])
    return jnp.concatenate(out, axis=1)


def _dwconv(full_ref, r, wdw_ref, y_ref, yrow, row_block, lane0=0, lane1=D_MODEL):
    win = row_block + CONV_HIST
    for c in range(lane0, lane1, LANES):
        window = full_ref[r:r + win, c:c + LANES]
        acc = None
        for rho in range(8):
            shifted = window if rho == 0 else pltpu.roll(window, win - rho, 0)
            for a in range(CONV_HIST // 8 + 1):
                j = 8 * a + rho - CONV_PAD
                if 0 <= j < CONV_W:
                    term = shifted[8 * a:8 * a + row_block] * wdw_ref[j:j + 1, c:c + LANES]
                    acc = term if acc is None else acc + term
        y_ref[yrow:yrow + row_block, c:c + LANES] = acc


def _conv_tail(y, bdw, lng, lnb):
    y = y + bdw
    mu = jnp.mean(y, axis=-1, keepdims=True)
    yc = y - mu
    var = jnp.mean(yc * yc, axis=-1, keepdims=True)
    z = yc * lax.rsqrt(var + EPS) * lng + lnb
    return z * jax.nn.sigmoid(z)


def _merge_out(x, h, attn_bf16, conv_act_bf16, wr_ref, wo_ref, wpw2_ref, wout_ref):
    g = jax.nn.sigmoid(_dot(h, wr_ref[:, REST_G:]))
    merged = (g[:, :D_MODEL] * _dot(attn_bf16, wo_ref[...])
              + g[:, D_MODEL:] * _dot(conv_act_bf16, wpw2_ref[...]))
    return x + _dot(merged.astype(BF16), wout_ref[...])


def _layer_prompt_kernel(sink_ref, x_ref, cq_ref, slq_ref, shq_ref, ck_ref, slk_ref, shk_ref,
                         nmix_ref, wq_ref, wr_ref, wo_ref, wdw_ref,
                         bdw_ref, lng_ref, lnb_ref, wpw2_ref, wout_ref,
                         nmlp_ref, wup_ref, wdown_ref, nfin_ref,
                         xo_ref, kkeep_ref, vkeep_ref, ckeep_ref,
                         kh_ref, vh_ref, afull_ref, q_ref, at_ref, y_ref, g_ref, xnext_ref, xcur_ref,
                         *, tile, tiles_per_seq, n_tiles, final):
    s = pl.program_id(0)
    t = jnp.minimum(s, n_tiles - 1) % tiles_per_seq

    @pl.when(s == 0)
    def _():
        xcur_ref[...] = jnp.zeros((tile, D_MODEL), F32)

    @pl.when(s > 0)
    def _():
        xcur_ref[...] = xnext_ref[...]

    @pl.when(t == 0)
    def _():
        kh_ref[0:WINDOW, :] = jnp.zeros((WINDOW, KV_WIDTH), BF16)
        vh_ref[0:WINDOW, :] = jnp.zeros((WINDOW, KV_WIDTH), BF16)
        afull_ref[0:CONV_HIST, :] = jnp.zeros((CONV_HIST, D_MODEL), F32)

    @pl.when(t > 0)
    def _():
        kh_ref[0:WINDOW, :] = kh_ref[tile:tile + WINDOW, :]
        vh_ref[0:WINDOW, :] = vh_ref[tile:tile + WINDOW, :]
        afull_ref[0:CONV_HIST, :] = afull_ref[tile:tile + CONV_HIST, :]

    band = WINDOW + CHUNK
    key_chunk = lax.broadcasted_iota(jnp.int32, (1, band), 1) // CHUNK

    h = _rms(x_ref[...], nmix_ref[...]).astype(BF16)
    for c0 in range(0, D_MODEL, MXU_WIDTH):
        c1 = c0 + MXU_WIDTH
        lin = _dot(h, wr_ref[:, REST_U + c0:REST_U + c1])
        gate = _dot(h, wr_ref[:, REST_U + D_MODEL + c0:REST_U + D_MODEL + c1])
        afull_ref[CONV_HIST:CONV_HIST + tile, c0:c1] = lin * jax.nn.sigmoid(gate)
        for r in range(0, tile, CONV_ROWS):
            _dwconv(afull_ref, r, wdw_ref, y_ref, r, CONV_ROWS, c0, c1)
    q_ref[...] = _rope(_dot(h, wq_ref[...]), cq_ref[...], slq_ref[...], shq_ref[...]).astype(BF16)
    k = _rope(_dot(h, wr_ref[:, REST_K:REST_V]), ck_ref[...], slk_ref[...], shk_ref[...])
    v = _dot(h, wr_ref[:, REST_V:REST_U])
    kh_ref[WINDOW:WINDOW + tile, :] = k.astype(BF16)
    vh_ref[WINDOW:WINDOW + tile, :] = v.astype(BF16)
    kkeep_ref[0] = k[tile - WINDOW:, :]
    vkeep_ref[0] = v[tile - WINDOW:, :]
    ckeep_ref[0] = afull_ref[CONV_HIST + tile - CONV_KEEP:CONV_HIST + tile, :]
    g_ref[...] = jax.nn.sigmoid(_dot(h, wr_ref[:, REST_G:]))

    xo_ref[...] = _ffn_tile(xcur_ref[...], nmlp_ref, wup_ref, wdown_ref, nfin_ref, final)
    n_chunks = tile // CHUNK

    conv_act = _conv_tail(y_ref[...], bdw_ref[...], lng_ref[...], lnb_ref[...]).astype(BF16)
    for c in range(n_chunks):
        valid = None
        if c < WINDOW // CHUNK:
            valid = (t * n_chunks + c - WINDOW // CHUNK + key_chunk) >= 0
        o = _attend(q_ref[c * CHUNK:(c + 1) * CHUNK, :], kh_ref[c * CHUNK:c * CHUNK + band, :],
                    vh_ref[c * CHUNK:c * CHUNK + band, :], sink_ref, valid)
        at_ref[c * CHUNK:(c + 1) * CHUNK, :] = o.astype(BF16)

    merged = (g_ref[:, :D_MODEL] * _dot(at_ref[...], wo_ref[...])
              + g_ref[:, D_MODEL:] * _dot(conv_act, wpw2_ref[...]))
    xnext_ref[...] = x_ref[...] + _dot(merged.astype(BF16), wout_ref[...])


def _mixer_sample_kernel(sink_ref, x_ref, cq_ref, slq_ref, shq_ref, ck_ref, slk_ref, shk_ref,
                         kpast_ref, vpast_ref, cpast_ref,
                         nmix_ref, wq_ref, wr_ref, wo_ref, wdw_ref,
                         bdw_ref, lng_ref, lnb_ref, wpw2_ref, wout_ref,
                         xo_ref, knew_ref, vnew_ref, ckeep_ref,
                         kf_ref, vf_ref, afull_ref, at_ref, y_ref, *, streams, steps):
    x = x_ref[...]
    h = _rms(x, nmix_ref[...]).astype(BF16)
    q = _rope(_dot(h, wq_ref[...]), cq_ref[...], slq_ref[...], shq_ref[...]).astype(BF16)
    k = _rope(_dot(h, wr_ref[:, REST_K:REST_V]), ck_ref[...], slk_ref[...], shk_ref[...])
    v = _dot(h, wr_ref[:, REST_V:REST_U])
    knew_ref[...] = k
    vnew_ref[...] = v
    u = _dot(h, wr_ref[:, REST_U:REST_G])
    a = u[:, :D_MODEL] * jax.nn.sigmoid(u[:, D_MODEL:])

    for s in range(streams):
        rows = slice(s * steps, (s + 1) * steps)
        kf_ref[0:WINDOW, :] = kpast_ref[s].astype(BF16)
        vf_ref[0:WINDOW, :] = vpast_ref[s].astype(BF16)
        kf_ref[WINDOW:WINDOW + steps, :] = k[rows].astype(BF16)
        vf_ref[WINDOW:WINDOW + steps, :] = v[rows].astype(BF16)
        at_ref[rows, :] = _attend(q[rows], kf_ref[...], vf_ref[...], sink_ref, None).astype(BF16)

        afull_ref[0:CONV_HIST, :] = cpast_ref[s]
        afull_ref[CONV_HIST:CONV_HIST + steps, :] = a[rows]
        ckeep_ref[s] = afull_ref[CONV_HIST + steps - CONV_KEEP:CONV_HIST + steps, :]
        _dwconv(afull_ref, 0, wdw_ref, y_ref, s * steps, steps)

    conv_act = _conv_tail(y_ref[...], bdw_ref[...], lng_ref[...], lnb_ref[...]).astype(BF16)
    xo_ref[...] = _merge_out(x, h, at_ref[...], conv_act, wr_ref, wo_ref, wpw2_ref, wout_ref)


def _ffn_tile(x, nmlp_ref, wup_ref, wdown_ref, nfin_ref, final):
    hm = _rms(x, nmlp_ref[...]).astype(BF16)
    acc = x
    for c in range(0, D_FF, D_MODEL):
        r = jnp.maximum(_dot(hm, wup_ref[:, c:c + D_MODEL]), 0.0)
        acc = acc + _dot((r * r).astype(BF16), wdown_ref[c:c + D_MODEL, :])
    return _rms(acc, nfin_ref[...]) if final else acc


def _ffn_kernel(x_ref, nmlp_ref, wup_ref, wdown_ref, nfin_ref, o_ref, *, final):
    o_ref[...] = _ffn_tile(x_ref[...], nmlp_ref, wup_ref, wdown_ref, nfin_ref, final)


def _const_spec(shape):
    nd = len(shape)
    return pl.BlockSpec(shape, lambda *_: (0,) * nd, pipeline_mode=pl.Buffered(1))


def _layer_spec(shape, layer):
    nd = len(shape)
    return pl.BlockSpec((None,) + shape, lambda *_: (layer,) + (0,) * nd, pipeline_mode=pl.Buffered(1))


_SMEM_SPEC = pl.BlockSpec(memory_space=pltpu.SMEM)


def _mixer_weight_specs(layer):
    return [
        _layer_spec((1, D_MODEL), layer),
        _layer_spec((D_MODEL, D_MODEL), layer),
        _layer_spec((D_MODEL, REST_WIDTH), layer),
        _layer_spec((D_MODEL, D_MODEL), layer),
        _layer_spec((CONV_W, D_MODEL), layer),
        _layer_spec((1, D_MODEL), layer),
        _layer_spec((1, D_MODEL), layer),
        _layer_spec((1, D_MODEL), layer),
        _layer_spec((D_MODEL, D_MODEL), layer),
        _layer_spec((D_MODEL, D_MODEL), layer),
    ]


def _ffn_weight_specs(layer):
    return [_layer_spec((1, D_MODEL), layer), _layer_spec((D_MODEL, D_FF), layer),
            _layer_spec((D_FF, D_MODEL), layer), _const_spec((1, D_MODEL))]


def _layer_prompt(x, tabs, sink, weights, ffn_weights, layer, final):
    batch, seq, _ = x.shape
    tile = PROMPT_TILE
    tiles_per_seq = seq // tile
    n_tiles = batch * tiles_per_seq
    mixer_tile = lambda s: jnp.minimum(s, n_tiles - 1)
    ffn_tile = lambda s: jnp.maximum(s - 1, 0)
    x_spec = pl.BlockSpec((tile, D_MODEL), lambda s: (mixer_tile(s), 0))
    out_spec = pl.BlockSpec((tile, D_MODEL), lambda s: (ffn_tile(s), 0))
    tab_spec = pl.BlockSpec((tile, LANES), lambda s: (mixer_tile(s) % tiles_per_seq, 0))
    keep_spec = lambda rows, width: pl.BlockSpec(
        (1, rows, width), lambda s: (mixer_tile(s) // tiles_per_seq, 0, 0))
    y, kk, vk, ck = pl.pallas_call(
        functools.partial(_layer_prompt_kernel, tile=tile,
                          tiles_per_seq=tiles_per_seq, n_tiles=n_tiles, final=final),
        grid=(n_tiles + 1,),
        in_specs=([_SMEM_SPEC, x_spec] + [tab_spec] * 6 + _mixer_weight_specs(layer)
                  + _ffn_weight_specs(layer)),
        out_specs=[out_spec, keep_spec(WINDOW, KV_WIDTH), keep_spec(WINDOW, KV_WIDTH),
                   keep_spec(CONV_KEEP, D_MODEL)],
        out_shape=[jax.ShapeDtypeStruct((batch * seq, D_MODEL), F32),
                   jax.ShapeDtypeStruct((batch, WINDOW, KV_WIDTH), F32),
                   jax.ShapeDtypeStruct((batch, WINDOW, KV_WIDTH), F32),
                   jax.ShapeDtypeStruct((batch, CONV_KEEP, D_MODEL), F32)],
        scratch_shapes=[pltpu.VMEM((WINDOW + tile, KV_WIDTH), BF16),
                        pltpu.VMEM((WINDOW + tile, KV_WIDTH), BF16),
                        pltpu.VMEM((CONV_HIST + tile, D_MODEL), F32),
                        pltpu.VMEM((tile, D_MODEL), BF16),
                        pltpu.VMEM((tile, D_MODEL), BF16),
                        pltpu.VMEM((tile, D_MODEL), F32),
                        pltpu.VMEM((tile, 2 * D_MODEL), F32),
                        pltpu.VMEM((tile, D_MODEL), F32),
                        pltpu.VMEM((tile, D_MODEL), F32)],
        compiler_params=pltpu.CompilerParams(
            dimension_semantics=("arbitrary",), vmem_limit_bytes=VMEM_LIMIT),
        name="layer_prompt",
    )(sink, x.reshape(batch * seq, D_MODEL), *tabs, *weights, *ffn_weights)
    return y.reshape(batch, seq, D_MODEL), kk, vk, ck


def _mixer_sample(x, tabs, kpast, vpast, cpast, sink, weights, layer):
    streams, steps, _ = x.shape
    rows = streams * steps
    full = lambda shape: pl.BlockSpec(shape, lambda i: (0,) * len(shape))
    return pl.pallas_call(
        functools.partial(_mixer_sample_kernel, streams=streams, steps=steps),
        grid=(1,),
        in_specs=[_SMEM_SPEC, full((rows, D_MODEL))] + [full((rows, LANES))] * 6
        + [full((streams, WINDOW, KV_WIDTH)), full((streams, WINDOW, KV_WIDTH)),
           full((streams, CONV_HIST, D_MODEL))] + _mixer_weight_specs(layer),
        out_specs=[full((rows, D_MODEL)), full((rows, KV_WIDTH)), full((rows, KV_WIDTH)),
                   full((streams, CONV_KEEP, D_MODEL))],
        out_shape=[jax.ShapeDtypeStruct((rows, D_MODEL), F32),
                   jax.ShapeDtypeStruct((rows, KV_WIDTH), F32),
                   jax.ShapeDtypeStruct((rows, KV_WIDTH), F32),
                   jax.ShapeDtypeStruct((streams, CONV_KEEP, D_MODEL), F32)],
        scratch_shapes=[pltpu.VMEM((WINDOW + steps, KV_WIDTH), BF16),
                        pltpu.VMEM((WINDOW + steps, KV_WIDTH), BF16),
                        pltpu.VMEM((CONV_HIST + steps, D_MODEL), F32),
                        pltpu.VMEM((rows, D_MODEL), BF16),
                        pltpu.VMEM((rows, D_MODEL), F32)],
        compiler_params=pltpu.CompilerParams(
            dimension_semantics=("arbitrary",), vmem_limit_bytes=VMEM_LIMIT),
        name="mixer_sample",
    )(sink, x.reshape(rows, D_MODEL), *tabs, kpast, vpast, cpast, *weights)


def _ffn(x, nmlp, wup, wdown, nfin, layer, final, name):
    rows = x.shape[0]
    tile = min(FFN_TILE, rows)
    row_spec = pl.BlockSpec((tile, D_MODEL), lambda i: (i, 0))
    return pl.pallas_call(
        functools.partial(_ffn_kernel, final=final),
        grid=(rows // tile,),
        in_specs=[row_spec] + _ffn_weight_specs(layer),
        out_specs=row_spec,
        out_shape=jax.ShapeDtypeStruct((rows, D_MODEL), F32),
        compiler_params=pltpu.CompilerParams(
            dimension_semantics=("arbitrary",), vmem_limit_bytes=VMEM_LIMIT),
        name=name,
    )(x, nmlp, wup, wdown, nfin)


def _rope_tables(pos):
    half = HEAD_DIM // 2
    inv_freq = ROPE_THETA ** (-jnp.arange(half, dtype=F32) / half)
    ang = pos.astype(F32)[:, None] * inv_freq[None, :]
    cos, sin = jnp.cos(ang), jnp.sin(ang)
    zero = jnp.zeros_like(sin)
    reps = LANES // HEAD_DIM
    cos_t = jnp.tile(jnp.concatenate([cos, cos], axis=1), (1, reps))
    sin_lo = jnp.tile(jnp.concatenate([-sin, zero], axis=1), (1, reps))
    sin_hi = jnp.tile(jnp.concatenate([zero, sin], axis=1), (1, reps))
    scale = HEAD_DIM ** -0.5
    return (cos_t * scale, sin_lo * scale, sin_hi * scale, cos_t, sin_lo, sin_hi)


def _slot_major(n):
    j, i, d = jnp.meshgrid(jnp.arange(GROUP), jnp.arange(N_KV_HEADS), jnp.arange(HEAD_DIM), indexing="ij")
    return ((GROUP * i + j) * HEAD_DIM + d).reshape(n)


def kernel(x_prompt, x_sample, cache_k, cache_v, state_conv, norm_mix, w_in, sinks, w_o_attn,
           w_dw, b_dw, ln_conv_g, ln_conv_b, w_pw2, w_out, norm_mlp, w_up, w_down, norm_final):
    batch, seq, _ = x_prompt.shape
    streams, steps, _ = x_sample.shape
    aw = N_Q_HEADS * HEAD_DIM
    perm = _slot_major(aw)
    tabs_p = _rope_tables(jnp.arange(seq, dtype=jnp.int32))
    tabs_s = tuple(jnp.tile(tb, (streams, 1))
                   for tb in _rope_tables(PAST_LEN + jnp.arange(steps, dtype=jnp.int32)))
    rows = lambda p: p.reshape(p.shape[0], 1, -1)

    weights = (rows(norm_mix), w_in[:, :, :aw][:, :, perm].astype(BF16), w_in[:, :, aw:].astype(BF16),
               w_o_attn[:, perm, :].astype(BF16), w_dw, rows(b_dw), rows(ln_conv_g), rows(ln_conv_b),
               w_pw2.astype(BF16), w_out.astype(BF16))
    ffn_w = (rows(norm_mlp), w_up.astype(BF16), w_down.astype(BF16), norm_final.reshape(1, -1))

    xp = x_prompt
    xs = x_sample.reshape(streams * steps, D_MODEL)
    outs = [[] for _ in range(6)]
    for l in range(DEPTH):
        final = l == DEPTH - 1

        xp, kp, vp, cp = _layer_prompt(xp, tabs_p, sinks[l], weights, ffn_w, l, final)

        cpast = jnp.pad(state_conv[l], ((0, 0), (CONV_PAD, 0), (0, 0)))
        xs, kn, vn, cn = _mixer_sample(
            xs.reshape(streams, steps, D_MODEL), tabs_s,
            cache_k[l].reshape(streams, WINDOW, KV_WIDTH), cache_v[l].reshape(streams, WINDOW, KV_WIDTH),
            cpast, sinks[l], weights, l)
        xs = _ffn(xs, *ffn_w, l, final, "ffn_sample")

        heads = lambda z, n: z.reshape(-1, n, N_KV_HEADS, HEAD_DIM)
        for dst, val in zip(outs, (heads(kp, WINDOW), heads(vp, WINDOW), cp,
                                   heads(kn, steps), heads(vn, steps), cn)):
            dst.append(val)

    return (xp, xs.reshape(streams, steps, D_MODEL)) + tuple(jnp.stack(o) for o in outs)
```

```python
import functools

import jax
import jax.numpy as jnp
from jax import lax
from jax.experimental import pallas as pl
from jax.experimental.pallas import tpu as pltpu

D_MODEL = 1024
DEPTH = 2
PAST_LEN = 2048
CHUNK = 64
HEAD_DIM = 64
N_Q_HEADS = 16
N_KV_HEADS = 4
GROUP = N_Q_HEADS // N_KV_HEADS
KV_WIDTH = N_KV_HEADS * HEAD_DIM
WINDOW = 128
CONV_W = 31
CONV_KEEP = CONV_W - 1
D_FF = 4 * D_MODEL
REST_K = 0
REST_V = REST_K + KV_WIDTH
REST_U = REST_V + KV_WIDTH
REST_G = REST_U + 2 * D_MODEL
REST_WIDTH = REST_G + 2 * D_MODEL
ROPE_THETA = 10000.0
EPS = 1e-6
NEG_INF = -1e30

SCORE_SCALE = HEAD_DIM ** -0.5

LANES = 128
SUBLANES = 8
MXU_WIDTH = 256
CONV_HIST = 32
CONV_PAD = CONV_HIST - CONV_KEEP
CONV_ROWS = 128
VMEM_LIMIT = 56 * 1024 * 1024

PROMPT_TILE = 256
FFN_TILE = 512

F32 = jnp.float32
BF16 = jnp.bfloat16


def _rms(x, g):
    return x * lax.rsqrt(jnp.mean(x * x, axis=-1, keepdims=True) + EPS) * g


def _dot(a, b):
    return jnp.dot(a, b, preferred_element_type=F32)


def _rope(x, cos, sin_lo, sin_hi):
    outs = []
    for b in range(x.shape[1] // LANES):
        xb = x[:, b * LANES:(b + 1) * LANES]
        up = pltpu.roll(xb, LANES - HEAD_DIM // 2, 1)
        dn = pltpu.roll(xb, HEAD_DIM // 2, 1)
        outs.append(xb * cos + up * sin_lo + dn * sin_hi)
    return jnp.concatenate(outs, axis=1)


def _attend(q, k, v, sink_ref, valid):
    nq = q.shape[0]
    lane_head = lax.broadcasted_iota(jnp.int32, (nq, KV_WIDTH), 1) // HEAD_DIM
    out = [None] * GROUP
    for i in range(N_KV_HEADS):
        sel = lane_head == i
        qm = jnp.concatenate(
            [jnp.where(sel, q[:, j * KV_WIDTH:(j + 1) * KV_WIDTH], jnp.zeros((), q.dtype))
             for j in range(GROUP)], axis=0)
        s = lax.dot_general(qm, k, (((1,), (1,)), ((), ())), preferred_element_type=F32)
        if valid is not None:
            s = jnp.where(valid, s, NEG_INF)
        ps = []
        for j in range(GROUP):
            sj = s[j * nq:(j + 1) * nq]
            sink = sink_ref[GROUP * i + j]
            m = jnp.maximum(jnp.max(sj, axis=-1, keepdims=True), sink)
            p = jnp.exp(sj - m)
            den = jnp.sum(p, axis=-1, keepdims=True) + jnp.exp(sink - m)
            ps.append((p * (1.0 / den)).astype(BF16))
        o = _dot(jnp.concatenate(ps, axis=0), v)
        for j in range(GROUP):
            oj = o[j * nq:(j + 1) * nq]
            out[j] = oj if i == 0 else jnp.where(sel, oj, out[j])
    return jnp.concatenate(out, axis=1)


def _dwconv(full_ref, r, wdw_ref, y_ref, yrow, row_block, lane0=0, lane1=D_MODEL):
    win = row_block + CONV_HIST
    for c in range(lane0, lane1, LANES):
        window = full_ref[r:r + win, c:c + LANES]
        acc = None
        for rho in range(SUBLANES):
            shifted = window if rho == 0 else pltpu.roll(window, win - rho, 0)
            for a in range(CONV_HIST // SUBLANES + 1):
                j = SUBLANES * a + rho - CONV_PAD
                if 0 <= j < CONV_W:
                    term = (shifted[SUBLANES * a:SUBLANES * a + row_block]
                            * wdw_ref[j:j + 1, c:c + LANES])
                    acc = term if acc is None else acc + term
        y_ref[yrow:yrow + row_block, c:c + LANES] = acc


def _conv_tail(y, bdw, lng, lnb):
    y = y + bdw
    mu = jnp.mean(y, axis=-1, keepdims=True)
    yc = y - mu
    var = jnp.mean(yc * yc, axis=-1, keepdims=True)
    z = yc * lax.rsqrt(var + EPS) * lng + lnb
    return z * jax.nn.sigmoid(z)


def _merge_out(x, h, attn_bf16, conv_act_bf16, wr_ref, wo_ref, wpw2_ref, wout_ref):
    g = jax.nn.sigmoid(_dot(h, wr_ref[:, REST_G:]))
    merged = (g[:, :D_MODEL] * _dot(attn_bf16, wo_ref[...])
              + g[:, D_MODEL:] * _dot(conv_act_bf16, wpw2_ref[...]))
    return x + _dot(merged.astype(BF16), wout_ref[...])


def _layer_prompt_kernel(sink_ref, x_ref, cos_ref, sinlo_ref, sinhi_ref,
                         nmix_ref, wq_ref, wr_ref, wo_ref, wdw_ref,
                         bdw_ref, lng_ref, lnb_ref, wpw2_ref, wout_ref,
                         nmlp_ref, wup_ref, wdown_ref, nfin_ref,
                         xo_ref, kkeep_ref, vkeep_ref, ckeep_ref,
                         kh_ref, vh_ref, afull_ref, q_ref, at_ref, y_ref, g_ref, xnext_ref, xcur_ref,
                         *, tile, tiles_per_seq, n_tiles, final):
    s = pl.program_id(0)
    t = jnp.minimum(s, n_tiles - 1) % tiles_per_seq

    @pl.when(s == 0)
    def _():
        xcur_ref[...] = jnp.zeros((tile, D_MODEL), F32)

    @pl.when(s > 0)
    def _():
        xcur_ref[...] = xnext_ref[...]

    @pl.when(t == 0)
    def _():
        kh_ref[0:WINDOW, :] = jnp.zeros((WINDOW, KV_WIDTH), BF16)
        vh_ref[0:WINDOW, :] = jnp.zeros((WINDOW, KV_WIDTH), BF16)
        afull_ref[0:CONV_HIST, :] = jnp.zeros((CONV_HIST, D_MODEL), F32)

    @pl.when(t > 0)
    def _():
        kh_ref[0:WINDOW, :] = kh_ref[tile:tile + WINDOW, :]
        vh_ref[0:WINDOW, :] = vh_ref[tile:tile + WINDOW, :]
        afull_ref[0:CONV_HIST, :] = afull_ref[tile:tile + CONV_HIST, :]

    band = WINDOW + CHUNK
    key_chunk = lax.broadcasted_iota(jnp.int32, (1, band), 1) // CHUNK

    h = _rms(x_ref[...], nmix_ref[...]).astype(BF16)
    for c0 in range(0, D_MODEL, MXU_WIDTH):
        c1 = c0 + MXU_WIDTH
        lin = _dot(h, wr_ref[:, REST_U + c0:REST_U + c1])
        gate = _dot(h, wr_ref[:, REST_U + D_MODEL + c0:REST_U + D_MODEL + c1])
        afull_ref[CONV_HIST:CONV_HIST + tile, c0:c1] = lin * jax.nn.sigmoid(gate)
        for r in range(0, tile, CONV_ROWS):
            _dwconv(afull_ref, r, wdw_ref, y_ref, r, CONV_ROWS, c0, c1)
    rope_k = (cos_ref[...], sinlo_ref[...], sinhi_ref[...])
    rope_q = tuple(tb * SCORE_SCALE for tb in rope_k)
    q_ref[...] = _rope(_dot(h, wq_ref[...]), *rope_q).astype(BF16)
    k = _rope(_dot(h, wr_ref[:, REST_K:REST_V]), *rope_k)
    v = _dot(h, wr_ref[:, REST_V:REST_U])
    kh_ref[WINDOW:WINDOW + tile, :] = k.astype(BF16)
    vh_ref[WINDOW:WINDOW + tile, :] = v.astype(BF16)
    kkeep_ref[0] = k[tile - WINDOW:, :]
    vkeep_ref[0] = v[tile - WINDOW:, :]
    ckeep_ref[0] = afull_ref[CONV_HIST + tile - CONV_KEEP:CONV_HIST + tile, :]
    g_ref[...] = jax.nn.sigmoid(_dot(h, wr_ref[:, REST_G:]))

    xo_ref[...] = _ffn_tile(xcur_ref[...], nmlp_ref, wup_ref, wdown_ref, nfin_ref, final)
    n_chunks = tile // CHUNK

    conv_act = _conv_tail(y_ref[...], bdw_ref[...], lng_ref[...], lnb_ref[...]).astype(BF16)
    for c in range(n_chunks):
        valid = None
        if c < WINDOW // CHUNK:
            valid = (t * n_chunks + c - WINDOW // CHUNK + key_chunk) >= 0
        o = _attend(q_ref[c * CHUNK:(c + 1) * CHUNK, :], kh_ref[c * CHUNK:c * CHUNK + band, :],
                    vh_ref[c * CHUNK:c * CHUNK + band, :], sink_ref, valid)
        at_ref[c * CHUNK:(c + 1) * CHUNK, :] = o.astype(BF16)

    merged = (g_ref[:, :D_MODEL] * _dot(at_ref[...], wo_ref[...])
              + g_ref[:, D_MODEL:] * _dot(conv_act, wpw2_ref[...]))
    xnext_ref[...] = x_ref[...] + _dot(merged.astype(BF16), wout_ref[...])


def _mixer_sample_kernel(sink_ref, x_ref, cos_ref, sinlo_ref, sinhi_ref,
                         kpast_ref, vpast_ref, cpast_ref,
                         nmix_ref, wq_ref, wr_ref, wo_ref, wdw_ref,
                         bdw_ref, lng_ref, lnb_ref, wpw2_ref, wout_ref,
                         xo_ref, knew_ref, vnew_ref, ckeep_ref,
                         kf_ref, vf_ref, afull_ref, at_ref, y_ref, *, streams, steps):
    x = x_ref[...]
    h = _rms(x, nmix_ref[...]).astype(BF16)
    rope_k = (cos_ref[...], sinlo_ref[...], sinhi_ref[...])
    rope_q = tuple(tb * SCORE_SCALE for tb in rope_k)
    q = _rope(_dot(h, wq_ref[...]), *rope_q).astype(BF16)
    k = _rope(_dot(h, wr_ref[:, REST_K:REST_V]), *rope_k)
    v = _dot(h, wr_ref[:, REST_V:REST_U])
    knew_ref[...] = k
    vnew_ref[...] = v
    u = _dot(h, wr_ref[:, REST_U:REST_G])
    a = u[:, :D_MODEL] * jax.nn.sigmoid(u[:, D_MODEL:])

    for s in range(streams):
        rows = slice(s * steps, (s + 1) * steps)
        kf_ref[0:WINDOW, :] = kpast_ref[s].astype(BF16)
        vf_ref[0:WINDOW, :] = vpast_ref[s].astype(BF16)
        kf_ref[WINDOW:WINDOW + steps, :] = k[rows].astype(BF16)
        vf_ref[WINDOW:WINDOW + steps, :] = v[rows].astype(BF16)
        at_ref[rows, :] = _attend(q[rows], kf_ref[...], vf_ref[...], sink_ref, None).astype(BF16)

        afull_ref[0:CONV_HIST, :] = cpast_ref[s]
        afull_ref[CONV_HIST:CONV_HIST + steps, :] = a[rows]
        ckeep_ref[s] = afull_ref[CONV_HIST + steps - CONV_KEEP:CONV_HIST + steps, :]
        _dwconv(afull_ref, 0, wdw_ref, y_ref, s * steps, steps)

    conv_act = _conv_tail(y_ref[...], bdw_ref[...], lng_ref[...], lnb_ref[...]).astype(BF16)
    xo_ref[...] = _merge_out(x, h, at_ref[...], conv_act, wr_ref, wo_ref, wpw2_ref, wout_ref)


def _ffn_tile(x, nmlp_ref, wup_ref, wdown_ref, nfin_ref, final):
    hm = _rms(x, nmlp_ref[...]).astype(BF16)
    acc = x
    for c in range(0, D_FF, D_MODEL):
        r = jnp.maximum(_dot(hm, wup_ref[:, c:c + D_MODEL]), 0.0)
        acc = acc + _dot((r * r).astype(BF16), wdown_ref[c:c + D_MODEL, :])
    return _rms(acc, nfin_ref[...]) if final else acc


def _ffn_kernel(x_ref, nmlp_ref, wup_ref, wdown_ref, nfin_ref, o_ref, *, final):
    o_ref[...] = _ffn_tile(x_ref[...], nmlp_ref, wup_ref, wdown_ref, nfin_ref, final)


def _const_spec(shape):
    nd = len(shape)
    return pl.BlockSpec(shape, lambda *_: (0,) * nd, pipeline_mode=pl.Buffered(1))


def _layer_spec(shape, layer):
    nd = len(shape)
    return pl.BlockSpec((None,) + shape, lambda *_: (layer,) + (0,) * nd, pipeline_mode=pl.Buffered(1))


_SMEM_SPEC = pl.BlockSpec(memory_space=pltpu.SMEM)


def _mixer_weight_specs(layer):
    return [
        _layer_spec((1, D_MODEL), layer),
        _layer_spec((D_MODEL, D_MODEL), layer),
        _layer_spec((D_MODEL, REST_WIDTH), layer),
        _layer_spec((D_MODEL, D_MODEL), layer),
        _layer_spec((CONV_W, D_MODEL), layer),
        _layer_spec((1, D_MODEL), layer),
        _layer_spec((1, D_MODEL), layer),
        _layer_spec((1, D_MODEL), layer),
        _layer_spec((D_MODEL, D_MODEL), layer),
        _layer_spec((D_MODEL, D_MODEL), layer),
    ]


def _ffn_weight_specs(layer):
    return [_layer_spec((1, D_MODEL), layer), _layer_spec((D_MODEL, D_FF), layer),
            _layer_spec((D_FF, D_MODEL), layer), _const_spec((1, D_MODEL))]


def _layer_prompt(x, tabs, sink, weights, ffn_weights, layer, final):
    batch, seq, _ = x.shape
    tile = PROMPT_TILE
    tiles_per_seq = seq // tile
    n_tiles = batch * tiles_per_seq
    mixer_tile = lambda s: jnp.minimum(s, n_tiles - 1)
    ffn_tile = lambda s: jnp.maximum(s - 1, 0)
    x_spec = pl.BlockSpec((tile, D_MODEL), lambda s: (mixer_tile(s), 0))
    out_spec = pl.BlockSpec((tile, D_MODEL), lambda s: (ffn_tile(s), 0))
    tab_spec = pl.BlockSpec((tile, LANES), lambda s: (mixer_tile(s) % tiles_per_seq, 0))
    keep_spec = lambda rows, width: pl.BlockSpec(
        (1, rows, width), lambda s: (mixer_tile(s) // tiles_per_seq, 0, 0))
    y, kk, vk, ck = pl.pallas_call(
        functools.partial(_layer_prompt_kernel, tile=tile,
                          tiles_per_seq=tiles_per_seq, n_tiles=n_tiles, final=final),
        grid=(n_tiles + 1,),
        in_specs=([_SMEM_SPEC, x_spec] + [tab_spec] * 3 + _mixer_weight_specs(layer)
                  + _ffn_weight_specs(layer)),
        out_specs=[out_spec, keep_spec(WINDOW, KV_WIDTH), keep_spec(WINDOW, KV_WIDTH),
                   keep_spec(CONV_KEEP, D_MODEL)],
        out_shape=[jax.ShapeDtypeStruct((batch * seq, D_MODEL), F32),
                   jax.ShapeDtypeStruct((batch, WINDOW, KV_WIDTH), F32),
                   jax.ShapeDtypeStruct((batch, WINDOW, KV_WIDTH), F32),
                   jax.ShapeDtypeStruct((batch, CONV_KEEP, D_MODEL), F32)],
        scratch_shapes=[pltpu.VMEM((WINDOW + tile, KV_WIDTH), BF16),
                        pltpu.VMEM((WINDOW + tile, KV_WIDTH), BF16),
                        pltpu.VMEM((CONV_HIST + tile, D_MODEL), F32),
                        pltpu.VMEM((tile, D_MODEL), BF16),
                        pltpu.VMEM((tile, D_MODEL), BF16),
                        pltpu.VMEM((tile, D_MODEL), F32),
                        pltpu.VMEM((tile, 2 * D_MODEL), F32),
                        pltpu.VMEM((tile, D_MODEL), F32),
                        pltpu.VMEM((tile, D_MODEL), F32)],
        compiler_params=pltpu.CompilerParams(
            dimension_semantics=("arbitrary",), vmem_limit_bytes=VMEM_LIMIT),
        name="layer_prompt",
    )(sink, x.reshape(batch * seq, D_MODEL), *tabs, *weights, *ffn_weights)
    return y.reshape(batch, seq, D_MODEL), kk, vk, ck


def _mixer_sample(x, tabs, kpast, vpast, cpast, sink, weights, layer):
    streams, steps, _ = x.shape
    rows = streams * steps
    full = lambda shape: pl.BlockSpec(shape, lambda i: (0,) * len(shape))
    return pl.pallas_call(
        functools.partial(_mixer_sample_kernel, streams=streams, steps=steps),
        grid=(1,),
        in_specs=[_SMEM_SPEC, full((rows, D_MODEL))] + [full((rows, LANES))] * 3
        + [full((streams, WINDOW, KV_WIDTH)), full((streams, WINDOW, KV_WIDTH)),
           full((streams, CONV_HIST, D_MODEL))] + _mixer_weight_specs(layer),
        out_specs=[full((rows, D_MODEL)), full((rows, KV_WIDTH)), full((rows, KV_WIDTH)),
                   full((streams, CONV_KEEP, D_MODEL))],
        out_shape=[jax.ShapeDtypeStruct((rows, D_MODEL), F32),
                   jax.ShapeDtypeStruct((rows, KV_WIDTH), F32),
                   jax.ShapeDtypeStruct((rows, KV_WIDTH), F32),
                   jax.ShapeDtypeStruct((streams, CONV_KEEP, D_MODEL), F32)],
        scratch_shapes=[pltpu.VMEM((WINDOW + steps, KV_WIDTH), BF16),
                        pltpu.VMEM((WINDOW + steps, KV_WIDTH), BF16),
                        pltpu.VMEM((CONV_HIST + steps, D_MODEL), F32),
                        pltpu.VMEM((rows, D_MODEL), BF16),
                        pltpu.VMEM((rows, D_MODEL), F32)],
        compiler_params=pltpu.CompilerParams(
            dimension_semantics=("arbitrary",), vmem_limit_bytes=VMEM_LIMIT),
        name="mixer_sample",
    )(sink, x.reshape(rows, D_MODEL), *tabs, kpast, vpast, cpast, *weights)


def _ffn(x, nmlp, wup, wdown, nfin, layer, final, name):
    rows = x.shape[0]
    tile = min(FFN_TILE, rows)
    row_spec = pl.BlockSpec((tile, D_MODEL), lambda i: (i, 0))
    return pl.pallas_call(
        functools.partial(_ffn_kernel, final=final),
        grid=(rows // tile,),
        in_specs=[row_spec] + _ffn_weight_specs(layer),
        out_specs=row_spec,
        out_shape=jax.ShapeDtypeStruct((rows, D_MODEL), F32),
        compiler_params=pltpu.CompilerParams(
            dimension_semantics=("arbitrary",), vmem_limit_bytes=VMEM_LIMIT),
        name=name,
    )(x, nmlp, wup, wdown, nfin)


def _rope_tables(pos):
    half = HEAD_DIM // 2
    inv_freq = ROPE_THETA ** (-jnp.arange(half, dtype=F32) / half)
    ang = pos.astype(F32)[:, None] * inv_freq[None, :]
    cos, sin = jnp.cos(ang), jnp.sin(ang)
    zero = jnp.zeros_like(sin)
    reps = LANES // HEAD_DIM
    cos_t = jnp.tile(jnp.concatenate([cos, cos], axis=1), (1, reps))
    sin_lo = jnp.tile(jnp.concatenate([-sin, zero], axis=1), (1, reps))
    sin_hi = jnp.tile(jnp.concatenate([zero, sin], axis=1), (1, reps))
    return cos_t, sin_lo, sin_hi


def _slot_major(n):
    j, i, d = jnp.meshgrid(jnp.arange(GROUP), jnp.arange(N_KV_HEADS), jnp.arange(HEAD_DIM), indexing="ij")
    return ((GROUP * i + j) * HEAD_DIM + d).reshape(n)


def kernel(x_prompt, x_sample, cache_k, cache_v, state_conv, norm_mix, w_in, sinks, w_o_attn,
           w_dw, b_dw, ln_conv_g, ln_conv_b, w_pw2, w_out, norm_mlp, w_up, w_down, norm_final):
    batch, seq, _ = x_prompt.shape
    streams, steps, _ = x_sample.shape
    aw = N_Q_HEADS * HEAD_DIM
    perm = _slot_major(aw)
    tabs_p = _rope_tables(jnp.arange(seq, dtype=jnp.int32))
    tabs_s = tuple(jnp.tile(tb, (streams, 1))
                   for tb in _rope_tables(PAST_LEN + jnp.arange(steps, dtype=jnp.int32)))
    rows = lambda p: p.reshape(p.shape[0], 1, -1)

    weights = (rows(norm_mix), jnp.take(w_in, perm, axis=2).astype(BF16), w_in[:, :, aw:].astype(BF16),
               w_o_attn[:, perm, :].astype(BF16), w_dw, rows(b_dw), rows(ln_conv_g), rows(ln_conv_b),
               w_pw2.astype(BF16), w_out.astype(BF16))
    ffn_w = (rows(norm_mlp), w_up.astype(BF16), w_down.astype(BF16), norm_final.reshape(1, -1))

    xp = x_prompt
    xs = x_sample.reshape(streams * steps, D_MODEL)
    outs = [[] for _ in range(6)]
    for l in range(DEPTH):
        final = l == DEPTH - 1

        xp, kp, vp, cp = _layer_prompt(xp, tabs_p, sinks[l], weights, ffn_w, l, final)

        cpast = jnp.pad(state_conv[l], ((0, 0), (CONV_PAD, 0), (0, 0)))
        xs, kn, vn, cn = _mixer_sample(
            xs.reshape(streams, steps, D_MODEL), tabs_s,
            cache_k[l].reshape(streams, WINDOW, KV_WIDTH), cache_v[l].reshape(streams, WINDOW, KV_WIDTH),
            cpast, sinks[l], weights, l)
        xs = _ffn(xs, *ffn_w, l, final, "ffn_sample")

        heads = lambda z, n: z.reshape(-1, n, N_KV_HEADS, HEAD_DIM)
        for dst, val in zip(outs, (heads(kp, WINDOW), heads(vp, WINDOW), cp,
                                   heads(kn, steps), heads(vn, steps), cn)):
            dst.append(val)

    return (xp, xs.reshape(streams, steps, D_MODEL)) + tuple(jnp.stack(o) for o in outs)
```

```python
import functools

import jax
import jax.numpy as jnp
from jax import lax
from jax.experimental import pallas as pl
from jax.experimental.pallas import tpu as pltpu

D_MODEL = 1024
DEPTH = 2
PAST_LEN = 2048
CHUNK = 64
HEAD_DIM = 64
N_Q_HEADS = 16
N_KV_HEADS = 4
GROUP = N_Q_HEADS // N_KV_HEADS
KV_WIDTH = N_KV_HEADS * HEAD_DIM
WINDOW = 128
CONV_W = 31
CONV_KEEP = CONV_W - 1
D_FF = 4 * D_MODEL
REST_K = 0
REST_V = REST_K + KV_WIDTH
REST_U = REST_V + KV_WIDTH
REST_G = REST_U + 2 * D_MODEL
REST_WIDTH = REST_G + 2 * D_MODEL
ROPE_THETA = 10000.0
EPS = 1e-6
NEG_INF = -1e30

SCORE_SCALE = HEAD_DIM ** -0.5

LANES = 128
SUBLANES = 8
MXU_WIDTH = 256
CONV_HIST = 32
CONV_PAD = CONV_HIST - CONV_KEEP
CONV_ROWS = 128
VMEM_LIMIT = 56 * 1024 * 1024

PROMPT_TILE = 256
FFN_TILE = 512

F32 = jnp.float32
BF16 = jnp.bfloat16


def _rms(x, g):
    return x * lax.rsqrt(jnp.mean(x * x, axis=-1, keepdims=True) + EPS) * g


def _dot(a, b):
    return jnp.dot(a, b, preferred_element_type=F32)


def _rope(x, cos, sin_lo, sin_hi):
    outs = []
    for b in range(x.shape[1] // LANES):
        xb = x[:, b * LANES:(b + 1) * LANES]
        up = pltpu.roll(xb, LANES - HEAD_DIM // 2, 1)
        dn = pltpu.roll(xb, HEAD_DIM // 2, 1)
        outs.append(xb * cos + up * sin_lo + dn * sin_hi)
    return jnp.concatenate(outs, axis=1)


def _attend(q, k, v, sink_ref, valid):
    nq = q.shape[0]
    lane_head = lax.broadcasted_iota(jnp.int32, (nq, KV_WIDTH), 1) // HEAD_DIM
    out = [None] * GROUP
    for i in range(N_KV_HEADS):
        sel = lane_head == i
        qm = jnp.concatenate(
            [jnp.where(sel, q[:, j * KV_WIDTH:(j + 1) * KV_WIDTH], jnp.zeros((), q.dtype))
             for j in range(GROUP)], axis=0)
        s = lax.dot_general(qm, k, (((1,), (1,)), ((), ())), preferred_element_type=F32)
        if valid is not None:
            s = jnp.where(valid, s, NEG_INF)
        ps = []
        for j in range(GROUP):
            sj = s[j * nq:(j + 1) * nq]
            sink = sink_ref[GROUP * i + j]
            m = jnp.maximum(jnp.max(sj, axis=-1, keepdims=True), sink)
            p = jnp.exp(sj - m)
            den = jnp.sum(p, axis=-1, keepdims=True) + jnp.exp(sink - m)
            ps.append((p * (1.0 / den)).astype(BF16))
        o = _dot(jnp.concatenate(ps, axis=0), v)
        for j in range(GROUP):
            oj = o[j * nq:(j + 1) * nq]
            out[j] = oj if i == 0 else jnp.where(sel, oj, out[j])
    return jnp.concatenate(out, axis=1)


def _dwconv(full_ref, r, wdw_ref, y_ref, yrow, row_block, lane0=0, lane1=D_MODEL):
    win = row_block + CONV_HIST
    for c in range(lane0, lane1, LANES):
        window = full_ref[r:r + win, c:c + LANES]
        acc = None
        for rho in range(SUBLANES):
            shifted = window if rho == 0 else pltpu.roll(window, win - rho, 0)
            for a in range(CONV_HIST // SUBLANES + 1):
                j = SUBLANES * a + rho - CONV_PAD
                if 0 <= j < CONV_W:
                    term = (shifted[SUBLANES * a:SUBLANES * a + row_block]
                            * wdw_ref[j:j + 1, c:c + LANES])
                    acc = term if acc is None else acc + term
        y_ref[yrow:yrow + row_block, c:c + LANES] = acc


def _conv_tail(y, bdw, lng, lnb):
    y = y + bdw
    mu = jnp.mean(y, axis=-1, keepdims=True)
    yc = y - mu
    var = jnp.mean(yc * yc, axis=-1, keepdims=True)
    z = yc * lax.rsqrt(var + EPS) * lng + lnb
    return z * jax.nn.sigmoid(z)


def _merge_out(x, h, attn_bf16, conv_act_bf16, wr_ref, wo_ref, wpw2_ref, wout_ref):
    g = jax.nn.sigmoid(_dot(h, wr_ref[:, REST_G:]))
    merged = (g[:, :D_MODEL] * _dot(attn_bf16, wo_ref[...])
              + g[:, D_MODEL:] * _dot(conv_act_bf16, wpw2_ref[...]))
    return x + _dot(merged.astype(BF16), wout_ref[...])


def _layer_prompt_kernel(sink_ref, x_ref, cos_ref, sinlo_ref, sinhi_ref,
                         nmix_ref, wq_ref, wr_ref, wo_ref, wdw_ref,
                         bdw_ref, lng_ref, lnb_ref, wpw2_ref, wout_ref,
                         nmlp_ref, wup_ref, wdown_ref, nfin_ref,
                         xo_ref, kkeep_ref, vkeep_ref, ckeep_ref,
                         kh_ref, vh_ref, afull_ref, q_ref, at_ref, y_ref, g_ref, xnext_ref, xcur_ref,
                         *, tile, tiles_per_seq, n_tiles, final):
    s = pl.program_id(0)
    t = jnp.minimum(s, n_tiles - 1) % tiles_per_seq

    @pl.when(s == 0)
    def _():
        xcur_ref[...] = jnp.zeros((tile, D_MODEL), F32)

    @pl.when(s > 0)
    def _():
        xcur_ref[...] = xnext_ref[...]

    @pl.when(t == 0)
    def _():
        kh_ref[0:WINDOW, :] = jnp.zeros((WINDOW, KV_WIDTH), BF16)
        vh_ref[0:WINDOW, :] = jnp.zeros((WINDOW, KV_WIDTH), BF16)
        afull_ref[0:CONV_HIST, :] = jnp.zeros((CONV_HIST, D_MODEL), F32)

    @pl.when(t > 0)
    def _():
        kh_ref[0:WINDOW, :] = kh_ref[tile:tile + WINDOW, :]
        vh_ref[0:WINDOW, :] = vh_ref[tile:tile + WINDOW, :]
        afull_ref[0:CONV_HIST, :] = afull_ref[tile:tile + CONV_HIST, :]

    band = WINDOW + CHUNK
    key_chunk = lax.broadcasted_iota(jnp.int32, (1, band), 1) // CHUNK

    h = _rms(x_ref[...], nmix_ref[...]).astype(BF16)
    for c0 in range(0, D_MODEL, MXU_WIDTH):
        c1 = c0 + MXU_WIDTH
        lin = _dot(h, wr_ref[:, REST_U + c0:REST_U + c1])
        gate = _dot(h, wr_ref[:, REST_U + D_MODEL + c0:REST_U + D_MODEL + c1])
        afull_ref[CONV_HIST:CONV_HIST + tile, c0:c1] = lin * jax.nn.sigmoid(gate)
        for r in range(0, tile, CONV_ROWS):
            _dwconv(afull_ref, r, wdw_ref, y_ref, r, CONV_ROWS, c0, c1)
    rope_k = (cos_ref[...], sinlo_ref[...], sinhi_ref[...])
    rope_q = tuple(tb * SCORE_SCALE for tb in rope_k)
    q_ref[...] = _rope(_dot(h, wq_ref[...]), *rope_q).astype(BF16)
    k = _rope(_dot(h, wr_ref[:, REST_K:REST_V]), *rope_k)
    v = _dot(h, wr_ref[:, REST_V:REST_U])
    kh_ref[WINDOW:WINDOW + tile, :] = k.astype(BF16)
    vh_ref[WINDOW:WINDOW + tile, :] = v.astype(BF16)
    kkeep_ref[0] = k[tile - WINDOW:, :]
    vkeep_ref[0] = v[tile - WINDOW:, :]
    ckeep_ref[0] = afull_ref[CONV_HIST + tile - CONV_KEEP:CONV_HIST + tile, :]
    g_ref[...] = jax.nn.sigmoid(_dot(h, wr_ref[:, REST_G:]))

    xo_ref[...] = _ffn_tile(xcur_ref[...], nmlp_ref, wup_ref, wdown_ref, nfin_ref, final)
    n_chunks = tile // CHUNK

    conv_act = _conv_tail(y_ref[...], bdw_ref[...], lng_ref[...], lnb_ref[...]).astype(BF16)
    for c in range(n_chunks):
        valid = None
        if c < WINDOW // CHUNK:
            valid = (t * n_chunks + c - WINDOW // CHUNK + key_chunk) >= 0
        o = _attend(q_ref[c * CHUNK:(c + 1) * CHUNK, :], kh_ref[c * CHUNK:c * CHUNK + band, :],
                    vh_ref[c * CHUNK:c * CHUNK + band, :], sink_ref, valid)
        at_ref[c * CHUNK:(c + 1) * CHUNK, :] = o.astype(BF16)

    merged = (g_ref[:, :D_MODEL] * _dot(at_ref[...], wo_ref[...])
              + g_ref[:, D_MODEL:] * _dot(conv_act, wpw2_ref[...]))
    xnext_ref[...] = x_ref[...] + _dot(merged.astype(BF16), wout_ref[...])


def _mixer_sample_kernel(sink_ref, x_ref, cos_ref, sinlo_ref, sinhi_ref,
                         kpast_ref, vpast_ref, cpast_ref,
                         nmix_ref, wq_ref, wr_ref, wo_ref, wdw_ref,
                         bdw_ref, lng_ref, lnb_ref, wpw2_ref, wout_ref,
                         xo_ref, knew_ref, vnew_ref, ckeep_ref,
                         kf_ref, vf_ref, afull_ref, at_ref, y_ref, *, streams, steps):
    x = x_ref[...]
    h = _rms(x, nmix_ref[...]).astype(BF16)
    rope_k = (cos_ref[...], sinlo_ref[...], sinhi_ref[...])
    rope_q = tuple(tb * SCORE_SCALE for tb in rope_k)
    q = _rope(_dot(h, wq_ref[...]), *rope_q).astype(BF16)
    k = _rope(_dot(h, wr_ref[:, REST_K:REST_V]), *rope_k)
    v = _dot(h, wr_ref[:, REST_V:REST_U])
    knew_ref[...] = k
    vnew_ref[...] = v
    u = _dot(h, wr_ref[:, REST_U:REST_G])
    a = u[:, :D_MODEL] * jax.nn.sigmoid(u[:, D_MODEL:])

    for s in range(streams):
        rows = slice(s * steps, (s + 1) * steps)
        kf_ref[0:WINDOW, :] = kpast_ref[s].astype(BF16)
        vf_ref[0:WINDOW, :] = vpast_ref[s].astype(BF16)
        kf_ref[WINDOW:WINDOW + steps, :] = k[rows].astype(BF16)
        vf_ref[WINDOW:WINDOW + steps, :] = v[rows].astype(BF16)
        at_ref[rows, :] = _attend(q[rows], kf_ref[...], vf_ref[...], sink_ref, None).astype(BF16)

        afull_ref[0:CONV_HIST, :] = cpast_ref[s]
        afull_ref[CONV_HIST:CONV_HIST + steps, :] = a[rows]
        ckeep_ref[s] = afull_ref[CONV_HIST + steps - CONV_KEEP:CONV_HIST + steps, :]
        _dwconv(afull_ref, 0, wdw_ref, y_ref, s * steps, steps)

    conv_act = _conv_tail(y_ref[...], bdw_ref[...], lng_ref[...], lnb_ref[...]).astype(BF16)
    xo_ref[...] = _merge_out(x, h, at_ref[...], conv_act, wr_ref, wo_ref, wpw2_ref, wout_ref)


def _ffn_tile(x, nmlp_ref, wup_ref, wdown_ref, nfin_ref, final):
    hm = _rms(x, nmlp_ref[...]).astype(BF16)
    acc = x
    for c in range(0, D_FF, D_MODEL):
        r = jnp.maximum(_dot(hm, wup_ref[:, c:c + D_MODEL]), 0.0)
        acc = acc + _dot((r * r).astype(BF16), wdown_ref[c:c + D_MODEL, :])
    return _rms(acc, nfin_ref[...]) if final else acc


def _ffn_kernel(x_ref, nmlp_ref, wup_ref, wdown_ref, nfin_ref, o_ref, *, final):
    o_ref[...] = _ffn_tile(x_ref[...], nmlp_ref, wup_ref, wdown_ref, nfin_ref, final)


def _const_spec(shape):
    nd = len(shape)
    return pl.BlockSpec(shape, lambda *_: (0,) * nd, pipeline_mode=pl.Buffered(1))


def _layer_spec(shape, layer):
    nd = len(shape)
    return pl.BlockSpec((None,) + shape, lambda *_: (layer,) + (0,) * nd, pipeline_mode=pl.Buffered(1))


_SMEM_SPEC = pl.BlockSpec(memory_space=pltpu.SMEM)


def _mixer_weight_specs(layer):
    return [
        _layer_spec((1, D_MODEL), layer),
        _layer_spec((D_MODEL, D_MODEL), layer),
        _layer_spec((D_MODEL, REST_WIDTH), layer),
        _layer_spec((D_MODEL, D_MODEL), layer),
        _layer_spec((CONV_W, D_MODEL), layer),
        _layer_spec((1, D_MODEL), layer),
        _layer_spec((1, D_MODEL), layer),
        _layer_spec((1, D_MODEL), layer),
        _layer_spec((D_MODEL, D_MODEL), layer),
        _layer_spec((D_MODEL, D_MODEL), layer),
    ]


def _ffn_weight_specs(layer):
    return [_layer_spec((1, D_MODEL), layer), _layer_spec((D_MODEL, D_FF), layer),
            _layer_spec((D_FF, D_MODEL), layer), _const_spec((1, D_MODEL))]


def _layer_prompt(x, tabs, sink, weights, ffn_weights, layer, final):
    batch, seq, _ = x.shape
    tile = PROMPT_TILE
    tiles_per_seq = seq // tile
    n_tiles = batch * tiles_per_seq
    mixer_tile = lambda s: jnp.minimum(s, n_tiles - 1)
    ffn_tile = lambda s: jnp.maximum(s - 1, 0)
    x_spec = pl.BlockSpec((tile, D_MODEL), lambda s: (mixer_tile(s), 0))
    out_spec = pl.BlockSpec((tile, D_MODEL), lambda s: (ffn_tile(s), 0))
    tab_spec = pl.BlockSpec((tile, LANES), lambda s: (mixer_tile(s) % tiles_per_seq, 0))
    keep_spec = lambda rows, width: pl.BlockSpec(
        (1, rows, width), lambda s: (mixer_tile(s) // tiles_per_seq, 0, 0))
    y, kk, vk, ck = pl.pallas_call(
        functools.partial(_layer_prompt_kernel, tile=tile,
                          tiles_per_seq=tiles_per_seq, n_tiles=n_tiles, final=final),
        grid=(n_tiles + 1,),
        in_specs=([_SMEM_SPEC, x_spec] + [tab_spec] * 3 + _mixer_weight_specs(layer)
                  + _ffn_weight_specs(layer)),
        out_specs=[out_spec, keep_spec(WINDOW, KV_WIDTH), keep_spec(WINDOW, KV_WIDTH),
                   keep_spec(CONV_KEEP, D_MODEL)],
        out_shape=[jax.ShapeDtypeStruct((batch * seq, D_MODEL), F32),
                   jax.ShapeDtypeStruct((batch, WINDOW, KV_WIDTH), F32),
                   jax.ShapeDtypeStruct((batch, WINDOW, KV_WIDTH), F32),
                   jax.ShapeDtypeStruct((batch, CONV_KEEP, D_MODEL), F32)],
        scratch_shapes=[pltpu.VMEM((WINDOW + tile, KV_WIDTH), BF16),
                        pltpu.VMEM((WINDOW + tile, KV_WIDTH), BF16),
                        pltpu.VMEM((CONV_HIST + tile, D_MODEL), F32),
                        pltpu.VMEM((tile, D_MODEL), BF16),
                        pltpu.VMEM((tile, D_MODEL), BF16),
                        pltpu.VMEM((tile, D_MODEL), F32),
                        pltpu.VMEM((tile, 2 * D_MODEL), F32),
                        pltpu.VMEM((tile, D_MODEL), F32),
                        pltpu.VMEM((tile, D_MODEL), F32)],
        compiler_params=pltpu.CompilerParams(
            dimension_semantics=("arbitrary",), vmem_limit_bytes=VMEM_LIMIT),
        name="layer_prompt",
    )(sink, x.reshape(batch * seq, D_MODEL), *tabs, *weights, *ffn_weights)
    return y.reshape(batch, seq, D_MODEL), kk, vk, ck


def _mixer_sample(x, tabs, kpast, vpast, cpast, sink, weights, layer):
    streams, steps, _ = x.shape
    rows = streams * steps
    full = lambda shape: pl.BlockSpec(shape, lambda i: (0,) * len(shape))
    return pl.pallas_call(
        functools.partial(_mixer_sample_kernel, streams=streams, steps=steps),
        grid=(1,),
        in_specs=[_SMEM_SPEC, full((rows, D_MODEL))] + [full((rows, LANES))] * 3
        + [full((streams, WINDOW, KV_WIDTH)), full((streams, WINDOW, KV_WIDTH)),
           full((streams, CONV_HIST, D_MODEL))] + _mixer_weight_specs(layer),
        out_specs=[full((rows, D_MODEL)), full((rows, KV_WIDTH)), full((rows, KV_WIDTH)),
                   full((streams, CONV_KEEP, D_MODEL))],
        out_shape=[jax.ShapeDtypeStruct((rows, D_MODEL), F32),
                   jax.ShapeDtypeStruct((rows, KV_WIDTH), F32),
                   jax.ShapeDtypeStruct((rows, KV_WIDTH), F32),
                   jax.ShapeDtypeStruct((streams, CONV_KEEP, D_MODEL), F32)],
        scratch_shapes=[pltpu.VMEM((WINDOW + steps, KV_WIDTH), BF16),
                        pltpu.VMEM((WINDOW + steps, KV_WIDTH), BF16),
                        pltpu.VMEM((CONV_HIST + steps, D_MODEL), F32),
                        pltpu.VMEM((rows, D_MODEL), BF16),
                        pltpu.VMEM((rows, D_MODEL), F32)],
        compiler_params=pltpu.CompilerParams(
            dimension_semantics=("arbitrary",), vmem_limit_bytes=VMEM_LIMIT),
        name="mixer_sample",
    )(sink, x.reshape(rows, D_MODEL), *tabs, kpast, vpast, cpast, *weights)


def _ffn(x, nmlp, wup, wdown, nfin, layer, final, name):
    rows = x.shape[0]
    tile = min(FFN_TILE, rows)
    row_spec = pl.BlockSpec((tile, D_MODEL), lambda i: (i, 0))
    return pl.pallas_call(
        functools.partial(_ffn_kernel, final=final),
        grid=(rows // tile,),
        in_specs=[row_spec] + _ffn_weight_specs(layer),
        out_specs=row_spec,
        out_shape=jax.ShapeDtypeStruct((rows, D_MODEL), F32),
        compiler_params=pltpu.CompilerParams(
            dimension_semantics=("arbitrary",), vmem_limit_bytes=VMEM_LIMIT),
        name=name,
    )(x, nmlp, wup, wdown, nfin)


def _rope_tables(pos):
    half = HEAD_DIM // 2
    inv_freq = ROPE_THETA ** (-jnp.arange(half, dtype=F32) / half)
    ang = pos.astype(F32)[:, None] * inv_freq[None, :]
    cos, sin = jnp.cos(ang), jnp.sin(ang)
    zero = jnp.zeros_like(sin)
    reps = LANES // HEAD_DIM
    cos_t = jnp.tile(jnp.concatenate([cos, cos], axis=1), (1, reps))
    sin_lo = jnp.tile(jnp.concatenate([-sin, zero], axis=1), (1, reps))
    sin_hi = jnp.tile(jnp.concatenate([zero, sin], axis=1), (1, reps))
    return cos_t, sin_lo, sin_hi


def _slot_major(n):
    j, i, d = jnp.meshgrid(jnp.arange(GROUP), jnp.arange(N_KV_HEADS), jnp.arange(HEAD_DIM), indexing="ij")
    return ((GROUP * i + j) * HEAD_DIM + d).reshape(n)


def kernel(x_prompt, x_sample, cache_k, cache_v, state_conv, norm_mix, w_in, sinks, w_o_attn,
           w_dw, b_dw, ln_conv_g, ln_conv_b, w_pw2, w_out, norm_mlp, w_up, w_down, norm_final):
    batch, seq, _ = x_prompt.shape
    streams, steps, _ = x_sample.shape
    aw = N_Q_HEADS * HEAD_DIM
    perm = _slot_major(aw)
    tabs_p = _rope_tables(jnp.arange(seq, dtype=jnp.int32))
    tabs_s = tuple(jnp.tile(tb, (streams, 1))
                   for tb in _rope_tables(PAST_LEN + jnp.arange(steps, dtype=jnp.int32)))
    rows = lambda p: p.reshape(p.shape[0], 1, -1)

    weights = (rows(norm_mix), w_in[:, :, :aw][:, :, perm].astype(BF16), w_in[:, :, aw:].astype(BF16),
               w_o_attn[:, perm, :].astype(BF16), w_dw, rows(b_dw), rows(ln_conv_g), rows(ln_conv_b),
               w_pw2.astype(BF16), w_out.astype(BF16))
    ffn_w = (rows(norm_mlp), w_up.astype(BF16), w_down.astype(BF16), norm_final.reshape(1, -1))

    xp = x_prompt
    xs = x_sample.reshape(streams * steps, D_MODEL)
    outs = [[] for _ in range(6)]
    for l in range(DEPTH):
        final = l == DEPTH - 1

        xp, kp, vp, cp = _layer_prompt(xp, tabs_p, sinks[l], weights, ffn_w, l, final)

        cpast = jnp.pad(state_conv[l], ((0, 0), (CONV_PAD, 0), (0, 0)))
        xs, kn, vn, cn = _mixer_sample(
            xs.reshape(streams, steps, D_MODEL), tabs_s,
            cache_k[l].reshape(streams, WINDOW, KV_WIDTH), cache_v[l].reshape(streams, WINDOW, KV_WIDTH),
            cpast, sinks[l], weights, l)
        xs = _ffn(xs, *ffn_w, l, final, "ffn_sample")

        heads = lambda z, n: z.reshape(-1, n, N_KV_HEADS, HEAD_DIM)
        for dst, val in zip(outs, (heads(kp, WINDOW), heads(vp, WINDOW), cp,
                                   heads(kn, steps), heads(vn, steps), cn)):
            dst.append(val)

    return (xp, xs.reshape(streams, steps, D_MODEL)) + tuple(jnp.stack(o) for o in outs)
```

```python
import functools

import jax
import jax.numpy as jnp
from jax import lax
from jax.experimental import pallas as pl
from jax.experimental.pallas import tpu as pltpu

D_MODEL = 1024
DEPTH = 2
PAST_LEN = 2048
CHUNK = 64
HEAD_DIM = 64
N_Q_HEADS = 16
N_KV_HEADS = 4
GROUP = N_Q_HEADS // N_KV_HEADS
KV_WIDTH = N_KV_HEADS * HEAD_DIM
WINDOW = 128
CONV_W = 31
CONV_KEEP = CONV_W - 1
D_FF = 4 * D_MODEL
REST_K = 0
REST_V = REST_K + KV_WIDTH
REST_U = REST_V + KV_WIDTH
REST_G = REST_U + 2 * D_MODEL
REST_WIDTH = REST_G + 2 * D_MODEL
ROPE_THETA = 10000.0
EPS = 1e-6
NEG_INF = -1e30

SCORE_SCALE = HEAD_DIM ** -0.5

LANES = 128
SUBLANES = 8
MXU_WIDTH = 256
CONV_HIST = 32
CONV_PAD = CONV_HIST - CONV_KEEP
CONV_ROWS = 128
VMEM_LIMIT = 56 * 1024 * 1024

PROMPT_TILE = 256
FFN_TILE = 512

F32 = jnp.float32
BF16 = jnp.bfloat16


def _rms(x, g):
    return x * lax.rsqrt(jnp.mean(x * x, axis=-1, keepdims=True) + EPS) * g


def _dot(a, b):
    return jnp.dot(a, b, preferred_element_type=F32)


def _rope(x, cos, sin_lo, sin_hi):
    outs = []
    for b in range(x.shape[1] // LANES):
        xb = x[:, b * LANES:(b + 1) * LANES]
        up = pltpu.roll(xb, LANES - HEAD_DIM // 2, 1)
        dn = pltpu.roll(xb, HEAD_DIM // 2, 1)
        outs.append(xb * cos + up * sin_lo + dn * sin_hi)
    return jnp.concatenate(outs, axis=1)


def _attend(q, k, v, sink_ref, valid):
    nq = q.shape[0]
    lane_head = lax.broadcasted_iota(jnp.int32, (nq, KV_WIDTH), 1) // HEAD_DIM
    out = [None] * GROUP
    for i in range(N_KV_HEADS):
        sel = lane_head == i
        qm = jnp.concatenate(
            [jnp.where(sel, q[:, j * KV_WIDTH:(j + 1) * KV_WIDTH], jnp.zeros((), q.dtype))
             for j in range(GROUP)], axis=0)
        s = lax.dot_general(qm, k, (((1,), (1,)), ((), ())), preferred_element_type=F32)
        if valid is not None:
            s = jnp.where(valid, s, NEG_INF)
        ps = []
        for j in range(GROUP):
            sj = s[j * nq:(j + 1) * nq]
            sink = sink_ref[GROUP * i + j]
            m = jnp.maximum(jnp.max(sj, axis=-1, keepdims=True), sink)
            p = jnp.exp(sj - m)
            den = jnp.sum(p, axis=-1, keepdims=True) + jnp.exp(sink - m)
            ps.append((p * (1.0 / den)).astype(BF16))
        o = _dot(jnp.concatenate(ps, axis=0), v)
        for j in range(GROUP):
            oj = o[j * nq:(j + 1) * nq]
            out[j] = oj if i == 0 else jnp.where(sel, oj, out[j])
    return jnp.concatenate(out, axis=1)


def _dwconv(full_ref, r, wdw_ref, y_ref, yrow, row_block, lane0=0, lane1=D_MODEL):
    win = row_block + CONV_HIST
    for c in range(lane0, lane1, LANES):
        window = full_ref[r:r + win, c:c + LANES]
        acc = None
        for rho in range(SUBLANES):
            shifted = window if rho == 0 else pltpu.roll(window, win - rho, 0)
            for a in range(CONV_HIST // SUBLANES + 1):
                j = SUBLANES * a + rho - CONV_PAD
                if 0 <= j < CONV_W:
                    term = (shifted[SUBLANES * a:SUBLANES * a + row_block]
                            * wdw_ref[j:j + 1, c:c + LANES])
                    acc = term if acc is None else acc + term
        y_ref[yrow:yrow + row_block, c:c + LANES] = acc


def _conv_tail(y, bdw, lng, lnb):
    y = y + bdw
    mu = jnp.mean(y, axis=-1, keepdims=True)
    yc = y - mu
    var = jnp.mean(yc * yc, axis=-1, keepdims=True)
    z = yc * lax.rsqrt(var + EPS) * lng + lnb
    return z * jax.nn.sigmoid(z)


def _merge_out(x, h, attn_bf16, conv_act_bf16, wr_ref, wo_ref, wpw2_ref, wout_ref):
    g = jax.nn.sigmoid(_dot(h, wr_ref[:, REST_G:]))
    merged = (g[:, :D_MODEL] * _dot(attn_bf16, wo_ref[...])
              + g[:, D_MODEL:] * _dot(conv_act_bf16, wpw2_ref[...]))
    return x + _dot(merged.astype(BF16), wout_ref[...])


def _layer_prompt_kernel(sink_ref, x_ref, cos_ref, sinlo_ref, sinhi_ref,
                         nmix_ref, wq_ref, wr_ref, wo_ref, wdw_ref,
                         bdw_ref, lng_ref, lnb_ref, wpw2_ref, wout_ref,
                         nmlp_ref, wup_ref, wdown_ref, nfin_ref,
                         xo_ref, kkeep_ref, vkeep_ref, ckeep_ref,
                         kh_ref, vh_ref, afull_ref, q_ref, at_ref, y_ref, g_ref, xnext_ref, xcur_ref,
                         *, tile, tiles_per_seq, n_tiles, final):
    s = pl.program_id(0)
    t = jnp.minimum(s, n_tiles - 1) % tiles_per_seq

    @pl.when(s == 0)
    def _():
        xcur_ref[...] = jnp.zeros((tile, D_MODEL), F32)

    @pl.when(s > 0)
    def _():
        xcur_ref[...] = xnext_ref[...]

    @pl.when(t == 0)
    def _():
        kh_ref[0:WINDOW, :] = jnp.zeros((WINDOW, KV_WIDTH), BF16)
        vh_ref[0:WINDOW, :] = jnp.zeros((WINDOW, KV_WIDTH), BF16)
        afull_ref[0:CONV_HIST, :] = jnp.zeros((CONV_HIST, D_MODEL), F32)

    @pl.when(t > 0)
    def _():
        kh_ref[0:WINDOW, :] = kh_ref[tile:tile + WINDOW, :]
        vh_ref[0:WINDOW, :] = vh_ref[tile:tile + WINDOW, :]
        afull_ref[0:CONV_HIST, :] = afull_ref[tile:tile + CONV_HIST, :]

    band = WINDOW + CHUNK
    key_chunk = lax.broadcasted_iota(jnp.int32, (1, band), 1) // CHUNK

    h = _rms(x_ref[...], nmix_ref[...]).astype(BF16)
    for c0 in range(0, D_MODEL, MXU_WIDTH):
        c1 = c0 + MXU_WIDTH
        lin = _dot(h, wr_ref[:, REST_U + c0:REST_U + c1])
        gate = _dot(h, wr_ref[:, REST_U + D_MODEL + c0:REST_U + D_MODEL + c1])
        afull_ref[CONV_HIST:CONV_HIST + tile, c0:c1] = lin * jax.nn.sigmoid(gate)
        for r in range(0, tile, CONV_ROWS):
            _dwconv(afull_ref, r, wdw_ref, y_ref, r, CONV_ROWS, c0, c1)
    rope_k = (cos_ref[...], sinlo_ref[...], sinhi_ref[...])
    rope_q = tuple(tb * SCORE_SCALE for tb in rope_k)
    q_ref[...] = _rope(_dot(h, wq_ref[...]), *rope_q).astype(BF16)
    k = _rope(_dot(h, wr_ref[:, REST_K:REST_V]), *rope_k)
    v = _dot(h, wr_ref[:, REST_V:REST_U])
    kh_ref[WINDOW:WINDOW + tile, :] = k.astype(BF16)
    vh_ref[WINDOW:WINDOW + tile, :] = v.astype(BF16)
    kkeep_ref[0] = k[tile - WINDOW:, :]
    vkeep_ref[0] = v[tile - WINDOW:, :]
    ckeep_ref[0] = afull_ref[CONV_HIST + tile - CONV_KEEP:CONV_HIST + tile, :]
    g_ref[...] = jax.nn.sigmoid(_dot(h, wr_ref[:, REST_G:]))

    xo_ref[...] = _ffn_tile(xcur_ref[...], nmlp_ref, wup_ref, wdown_ref, nfin_ref, final)
    n_chunks = tile // CHUNK

    conv_act = _conv_tail(y_ref[...], bdw_ref[...], lng_ref[...], lnb_ref[...]).astype(BF16)
    for c in range(n_chunks):
        valid = None
        if c < WINDOW // CHUNK:
            valid = (t * n_chunks + c - WINDOW // CHUNK + key_chunk) >= 0
        o = _attend(q_ref[c * CHUNK:(c + 1) * CHUNK, :], kh_ref[c * CHUNK:c * CHUNK + band, :],
                    vh_ref[c * CHUNK:c * CHUNK + band, :], sink_ref, valid)
        at_ref[c * CHUNK:(c + 1) * CHUNK, :] = o.astype(BF16)

    merged = (g_ref[:, :D_MODEL] * _dot(at_ref[...], wo_ref[...])
              + g_ref[:, D_MODEL:] * _dot(conv_act, wpw2_ref[...]))
    xnext_ref[...] = x_ref[...] + _dot(merged.astype(BF16), wout_ref[...])


def _mixer_sample_kernel(sink_ref, x_ref, cos_ref, sinlo_ref, sinhi_ref,
                         kpast_ref, vpast_ref, cpast_ref,
                         nmix_ref, wq_ref, wr_ref, wo_ref, wdw_ref,
                         bdw_ref, lng_ref, lnb_ref, wpw2_ref, wout_ref,
                         xo_ref, knew_ref, vnew_ref, ckeep_ref,
                         kf_ref, vf_ref, afull_ref, at_ref, y_ref, *, streams, steps):
    x = x_ref[...]
    h = _rms(x, nmix_ref[...]).astype(BF16)
    rope_k = (cos_ref[...], sinlo_ref[...], sinhi_ref[...])
    rope_q = tuple(tb * SCORE_SCALE for tb in rope_k)
    q = _rope(_dot(h, wq_ref[...]), *rope_q).astype(BF16)
    k = _rope(_dot(h, wr_ref[:, REST_K:REST_V]), *rope_k)
    v = _dot(h, wr_ref[:, REST_V:REST_U])
    knew_ref[...] = k
    vnew_ref[...] = v
    u = _dot(h, wr_ref[:, REST_U:REST_G])
    a = u[:, :D_MODEL] * jax.nn.sigmoid(u[:, D_MODEL:])

    for s in range(streams):
        rows = slice(s * steps, (s + 1) * steps)
        kf_ref[0:WINDOW, :] = kpast_ref[s].astype(BF16)
        vf_ref[0:WINDOW, :] = vpast_ref[s].astype(BF16)
        kf_ref[WINDOW:WINDOW + steps, :] = k[rows].astype(BF16)
        vf_ref[WINDOW:WINDOW + steps, :] = v[rows].astype(BF16)
        at_ref[rows, :] = _attend(q[rows], kf_ref[...], vf_ref[...], sink_ref, None).astype(BF16)

        afull_ref[0:CONV_HIST, :] = cpast_ref[s]
        afull_ref[CONV_HIST:CONV_HIST + steps, :] = a[rows]
        ckeep_ref[s] = afull_ref[CONV_HIST + steps - CONV_KEEP:CONV_HIST + steps, :]
        _dwconv(afull_ref, 0, wdw_ref, y_ref, s * steps, steps)

    conv_act = _conv_tail(y_ref[...], bdw_ref[...], lng_ref[...], lnb_ref[...]).astype(BF16)
    xo_ref[...] = _merge_out(x, h, at_ref[...], conv_act, wr_ref, wo_ref, wpw2_ref, wout_ref)


def _ffn_tile(x, nmlp_ref, wup_ref, wdown_ref, nfin_ref, final):
    hm = _rms(x, nmlp_ref[...]).astype(BF16)
    acc = x
    for c in range(0, D_FF, D_MODEL):
        r = jnp.maximum(_dot(hm, wup_ref[:, c:c + D_MODEL]), 0.0)
        acc = acc + _dot((r * r).astype(BF16), wdown_ref[c:c + D_MODEL, :])
    return _rms(acc, nfin_ref[...]) if final else acc


def _ffn_kernel(x_ref, nmlp_ref, wup_ref, wdown_ref, nfin_ref, o_ref, *, final):
    o_ref[...] = _ffn_tile(x_ref[...], nmlp_ref, wup_ref, wdown_ref, nfin_ref, final)


def _const_spec(shape):
    nd = len(shape)
    return pl.BlockSpec(shape, lambda *_: (0,) * nd, pipeline_mode=pl.Buffered(1))


def _layer_spec(shape, layer):
    nd = len(shape)
    return pl.BlockSpec((None,) + shape, lambda *_: (layer,) + (0,) * nd, pipeline_mode=pl.Buffered(1))


_SMEM_SPEC = pl.BlockSpec(memory_space=pltpu.SMEM)


def _mixer_weight_specs(layer):
    return [
        _layer_spec((1, D_MODEL), layer),
        _layer_spec((D_MODEL, D_MODEL), layer),
        _layer_spec((D_MODEL, REST_WIDTH), layer),
        _layer_spec((D_MODEL, D_MODEL), layer),
        _layer_spec((CONV_W, D_MODEL), layer),
        _layer_spec((1, D_MODEL), layer),
        _layer_spec((1, D_MODEL), layer),
        _layer_spec((1, D_MODEL), layer),
        _layer_spec((D_MODEL, D_MODEL), layer),
        _layer_spec((D_MODEL, D_MODEL), layer),
    ]


def _ffn_weight_specs(layer):
    return [_layer_spec((1, D_MODEL), layer), _layer_spec((D_MODEL, D_FF), layer),
            _layer_spec((D_FF, D_MODEL), layer), _const_spec((1, D_MODEL))]


def _layer_prompt(x, tabs, sink, weights, ffn_weights, layer, final):
    batch, seq, _ = x.shape
    tile = PROMPT_TILE
    tiles_per_seq = seq // tile
    n_tiles = batch * tiles_per_seq
    mixer_tile = lambda s: jnp.minimum(s, n_tiles - 1)
    ffn_tile = lambda s: jnp.maximum(s - 1, 0)
    x_spec = pl.BlockSpec((tile, D_MODEL), lambda s: (mixer_tile(s), 0))
    out_spec = pl.BlockSpec((tile, D_MODEL), lambda s: (ffn_tile(s), 0))
    tab_spec = pl.BlockSpec((tile, LANES), lambda s: (mixer_tile(s) % tiles_per_seq, 0))
    keep_spec = lambda rows, width: pl.BlockSpec(
        (1, rows, width), lambda s: (mixer_tile(s) // tiles_per_seq, 0, 0))
    y, kk, vk, ck = pl.pallas_call(
        functools.partial(_layer_prompt_kernel, tile=tile,
                          tiles_per_seq=tiles_per_seq, n_tiles=n_tiles, final=final),
        grid=(n_tiles + 1,),
        in_specs=([_SMEM_SPEC, x_spec] + [tab_spec] * 3 + _mixer_weight_specs(layer)
                  + _ffn_weight_specs(layer)),
        out_specs=[out_spec, keep_spec(WINDOW, KV_WIDTH), keep_spec(WINDOW, KV_WIDTH),
                   keep_spec(CONV_KEEP, D_MODEL)],
        out_shape=[jax.ShapeDtypeStruct((batch * seq, D_MODEL), F32),
                   jax.ShapeDtypeStruct((batch, WINDOW, KV_WIDTH), F32),
                   jax.ShapeDtypeStruct((batch, WINDOW, KV_WIDTH), F32),
                   jax.ShapeDtypeStruct((batch, CONV_KEEP, D_MODEL), F32)],
        scratch_shapes=[pltpu.VMEM((WINDOW + tile, KV_WIDTH), BF16),
                        pltpu.VMEM((WINDOW + tile, KV_WIDTH), BF16),
                        pltpu.VMEM((CONV_HIST + tile, D_MODEL), F32),
                        pltpu.VMEM((tile, D_MODEL), BF16),
                        pltpu.VMEM((tile, D_MODEL), BF16),
                        pltpu.VMEM((tile, D_MODEL), F32),
                        pltpu.VMEM((tile, 2 * D_MODEL), F32),
                        pltpu.VMEM((tile, D_MODEL), F32),
                        pltpu.VMEM((tile, D_MODEL), F32)],
        compiler_params=pltpu.CompilerParams(
            dimension_semantics=("arbitrary",), vmem_limit_bytes=VMEM_LIMIT),
        name="layer_prompt",
    )(sink, x.reshape(batch * seq, D_MODEL), *tabs, *weights, *ffn_weights)
    return y.reshape(batch, seq, D_MODEL), kk, vk, ck


def _mixer_sample(x, tabs, kpast, vpast, cpast, sink, weights, layer):
    streams, steps, _ = x.shape
    rows = streams * steps
    full = lambda shape: pl.BlockSpec(shape, lambda i: (0,) * len(shape))
    return pl.pallas_call(
        functools.partial(_mixer_sample_kernel, streams=streams, steps=steps),
        grid=(1,),
        in_specs=[_SMEM_SPEC, full((rows, D_MODEL))] + [full((rows, LANES))] * 3
        + [full((streams, WINDOW, KV_WIDTH)), full((streams, WINDOW, KV_WIDTH)),
           full((streams, CONV_HIST, D_MODEL))] + _mixer_weight_specs(layer),
        out_specs=[full((rows, D_MODEL)), full((rows, KV_WIDTH)), full((rows, KV_WIDTH)),
                   full((streams, CONV_KEEP, D_MODEL))],
        out_shape=[jax.ShapeDtypeStruct((rows, D_MODEL), F32),
                   jax.ShapeDtypeStruct((rows, KV_WIDTH), F32),
                   jax.ShapeDtypeStruct((rows, KV_WIDTH), F32),
                   jax.ShapeDtypeStruct((streams, CONV_KEEP, D_MODEL), F32)],
        scratch_shapes=[pltpu.VMEM((WINDOW + steps, KV_WIDTH), BF16),
                        pltpu.VMEM((WINDOW + steps, KV_WIDTH), BF16),
                        pltpu.VMEM((CONV_HIST + steps, D_MODEL), F32),
                        pltpu.VMEM((rows, D_MODEL), BF16),
                        pltpu.VMEM((rows, D_MODEL), F32)],
        compiler_params=pltpu.CompilerParams(
            dimension_semantics=("arbitrary",), vmem_limit_bytes=VMEM_LIMIT),
        name="mixer_sample",
    )(sink, x.reshape(rows, D_MODEL), *tabs, kpast, vpast, cpast, *weights)


def _ffn(x, nmlp, wup, wdown, nfin, layer, final, name):
    rows = x.shape[0]
    tile = min(FFN_TILE, rows)
    row_spec = pl.BlockSpec((tile, D_MODEL), lambda i: (i, 0))
    return pl.pallas_call(
        functools.partial(_ffn_kernel, final=final),
        grid=(rows // tile,),
        in_specs=[row_spec] + _ffn_weight_specs(layer),
        out_specs=row_spec,
        out_shape=jax.ShapeDtypeStruct((rows, D_MODEL), F32),
        compiler_params=pltpu.CompilerParams(
            dimension_semantics=("arbitrary",), vmem_limit_bytes=VMEM_LIMIT),
        name=name,
    )(x, nmlp, wup, wdown, nfin)


def _rope_tables(pos):
    half = HEAD_DIM // 2
    inv_freq = ROPE_THETA ** (-jnp.arange(half, dtype=F32) / half)
    ang = pos.astype(F32)[:, None] * inv_freq[None, :]
    cos, sin = jnp.cos(ang), jnp.sin(ang)
    zero = jnp.zeros_like(sin)
    reps = LANES // HEAD_DIM
    cos_t = jnp.tile(jnp.concatenate([cos, cos], axis=1), (1, reps))
    sin_lo = jnp.tile(jnp.concatenate([-sin, zero], axis=1), (1, reps))
    sin_hi = jnp.tile(jnp.concatenate([zero, sin], axis=1), (1, reps))
    return cos_t, sin_lo, sin_hi


def _slot_major(w, axis):
    shape = w.shape
    w = w.reshape(shape[:axis] + (N_KV_HEADS, GROUP, HEAD_DIM) + shape[axis + 1:])
    return jnp.swapaxes(w, axis, axis + 1).reshape(shape)


def kernel(x_prompt, x_sample, cache_k, cache_v, state_conv, norm_mix, w_in, sinks, w_o_attn,
           w_dw, b_dw, ln_conv_g, ln_conv_b, w_pw2, w_out, norm_mlp, w_up, w_down, norm_final):
    batch, seq, _ = x_prompt.shape
    streams, steps, _ = x_sample.shape
    aw = N_Q_HEADS * HEAD_DIM
    tabs_p = _rope_tables(jnp.arange(seq, dtype=jnp.int32))
    tabs_s = tuple(jnp.tile(tb, (streams, 1))
                   for tb in _rope_tables(PAST_LEN + jnp.arange(steps, dtype=jnp.int32)))
    rows = lambda p: p.reshape(p.shape[0], 1, -1)

    weights = (rows(norm_mix), _slot_major(w_in[:, :, :aw], 2).astype(BF16), w_in[:, :, aw:].astype(BF16),
               _slot_major(w_o_attn, 1).astype(BF16), w_dw, rows(b_dw), rows(ln_conv_g), rows(ln_conv_b),
               w_pw2.astype(BF16), w_out.astype(BF16))
    ffn_w = (rows(norm_mlp), w_up.astype(BF16), w_down.astype(BF16), norm_final.reshape(1, -1))

    xp = x_prompt
    xs = x_sample.reshape(streams * steps, D_MODEL)
    outs = [[] for _ in range(6)]
    for l in range(DEPTH):
        final = l == DEPTH - 1

        xp, kp, vp, cp = _layer_prompt(xp, tabs_p, sinks[l], weights, ffn_w, l, final)

        cpast = jnp.pad(state_conv[l], ((0, 0), (CONV_PAD, 0), (0, 0)))
        xs, kn, vn, cn = _mixer_sample(
            xs.reshape(streams, steps, D_MODEL), tabs_s,
            cache_k[l].reshape(streams, WINDOW, KV_WIDTH), cache_v[l].reshape(streams, WINDOW, KV_WIDTH),
            cpast, sinks[l], weights, l)
        xs = _ffn(xs, *ffn_w, l, final, "ffn_sample")

        heads = lambda z, n: z.reshape(-1, n, N_KV_HEADS, HEAD_DIM)
        for dst, val in zip(outs, (heads(kp, WINDOW), heads(vp, WINDOW), cp,
                                   heads(kn, steps), heads(vn, steps), cn)):
            dst.append(val)

    return (xp, xs.reshape(streams, steps, D_MODEL)) + tuple(jnp.stack(o) for o in outs)
```

```python
import functools

import jax
import jax.numpy as jnp
from jax import lax
from jax.experimental import pallas as pl
from jax.experimental.pallas import tpu as pltpu

D_MODEL = 1024
DEPTH = 2
PAST_LEN = 2048
CHUNK = 64
HEAD_DIM = 64
N_Q_HEADS = 16
N_KV_HEADS = 4
GROUP = N_Q_HEADS // N_KV_HEADS
KV_WIDTH = N_KV_HEADS * HEAD_DIM
WINDOW = 128
CONV_W = 31
CONV_KEEP = CONV_W - 1
D_FF = 4 * D_MODEL
REST_K = 0
REST_V = REST_K + KV_WIDTH
REST_U = REST_V + KV_WIDTH
REST_G = REST_U + 2 * D_MODEL
REST_WIDTH = REST_G + 2 * D_MODEL
ROPE_THETA = 10000.0
EPS = 1e-6
NEG_INF = -1e30

SCORE_SCALE = HEAD_DIM ** -0.5

LANES = 128
SUBLANES = 8
MXU_WIDTH = 256
CONV_HIST = 32
CONV_PAD = CONV_HIST - CONV_KEEP
CONV_ROWS = 128
VMEM_LIMIT = 56 * 1024 * 1024

PROMPT_TILE = 256
FFN_TILE = 512

F32 = jnp.float32
BF16 = jnp.bfloat16


def _rms(x, g):
    return x * lax.rsqrt(jnp.mean(x * x, axis=-1, keepdims=True) + EPS) * g


def _dot(a, b):
    return jnp.dot(a, b, preferred_element_type=F32)


def _rope(x, cos, sin_lo, sin_hi):
    outs = []
    for b in range(x.shape[1] // LANES):
        xb = x[:, b * LANES:(b + 1) * LANES]
        up = pltpu.roll(xb, LANES - HEAD_DIM // 2, 1)
        dn = pltpu.roll(xb, HEAD_DIM // 2, 1)
        outs.append(xb * cos + up * sin_lo + dn * sin_hi)
    return jnp.concatenate(outs, axis=1)


def _attend(q, k, v, sink_ref, valid):
    nq = q.shape[0]
    lane_head = lax.broadcasted_iota(jnp.int32, (nq, KV_WIDTH), 1) // HEAD_DIM
    out = [None] * GROUP
    for i in range(N_KV_HEADS):
        sel = lane_head == i
        qm = jnp.concatenate(
            [jnp.where(sel, q[:, j * KV_WIDTH:(j + 1) * KV_WIDTH], jnp.zeros((), q.dtype))
             for j in range(GROUP)], axis=0)
        s = lax.dot_general(qm, k, (((1,), (1,)), ((), ())), preferred_element_type=F32)
        if valid is not None:
            s = jnp.where(valid, s, NEG_INF)
        ps = []
        for j in range(GROUP):
            sj = s[j * nq:(j + 1) * nq]
            sink = sink_ref[GROUP * i + j]
            m = jnp.maximum(jnp.max(sj, axis=-1, keepdims=True), sink)
            p = jnp.exp(sj - m)
            den = jnp.sum(p, axis=-1, keepdims=True) + jnp.exp(sink - m)
            ps.append((p * (1.0 / den)).astype(BF16))
        o = _dot(jnp.concatenate(ps, axis=0), v)
        for j in range(GROUP):
            oj = o[j * nq:(j + 1) * nq]
            out[j] = oj if i == 0 else jnp.where(sel, oj, out[j])
    return jnp.concatenate(out, axis=1)


def _dwconv(full_ref, r, wdw_ref, y_ref, yrow, row_block, lane0=0, lane1=D_MODEL):
    win = row_block + CONV_HIST
    for c in range(lane0, lane1, LANES):
        window = full_ref[r:r + win, c:c + LANES]
        acc = None
        for rho in range(SUBLANES):
            shifted = window if rho == 0 else pltpu.roll(window, win - rho, 0)
            for a in range(CONV_HIST // SUBLANES + 1):
                j = SUBLANES * a + rho - CONV_PAD
                if 0 <= j < CONV_W:
                    term = (shifted[SUBLANES * a:SUBLANES * a + row_block]
                            * wdw_ref[j:j + 1, c:c + LANES])
                    acc = term if acc is None else acc + term
        y_ref[yrow:yrow + row_block, c:c + LANES] = acc


def _conv_tail(y, bdw, lng, lnb):
    y = y + bdw
    mu = jnp.mean(y, axis=-1, keepdims=True)
    yc = y - mu
    var = jnp.mean(yc * yc, axis=-1, keepdims=True)
    z = yc * lax.rsqrt(var + EPS) * lng + lnb
    return z * jax.nn.sigmoid(z)


def _merge_out(x, h, attn_bf16, conv_act_bf16, wr_ref, wo_ref, wpw2_ref, wout_ref):
    g = jax.nn.sigmoid(_dot(h, wr_ref[:, REST_G:]))
    merged = (g[:, :D_MODEL] * _dot(attn_bf16, wo_ref[...])
              + g[:, D_MODEL:] * _dot(conv_act_bf16, wpw2_ref[...]))
    return x + _dot(merged.astype(BF16), wout_ref[...])


def _layer_prompt_kernel(sink_ref, x_ref, cos_ref, sinlo_ref, sinhi_ref,
                         nmix_ref, wq_ref, wr_ref, wo_ref, wdw_ref,
                         bdw_ref, lng_ref, lnb_ref, wpw2_ref, wout_ref,
                         nmlp_ref, wup_ref, wdown_ref, nfin_ref,
                         xo_ref, kkeep_ref, vkeep_ref, ckeep_ref,
                         kh_ref, vh_ref, afull_ref, q_ref, at_ref, y_ref, g_ref, xmid_ref,
                         *, tile, tiles_per_seq, n_tiles, final):
    s = pl.program_id(0)
    t = jnp.minimum(s, n_tiles - 1) % tiles_per_seq

    @pl.when(s == 0)
    def _():
        xmid_ref[...] = jnp.zeros((tile, D_MODEL), F32)

    @pl.when(t == 0)
    def _():
        kh_ref[0:WINDOW, :] = jnp.zeros((WINDOW, KV_WIDTH), BF16)
        vh_ref[0:WINDOW, :] = jnp.zeros((WINDOW, KV_WIDTH), BF16)
        afull_ref[0:CONV_HIST, :] = jnp.zeros((CONV_HIST, D_MODEL), F32)

    @pl.when(t > 0)
    def _():
        kh_ref[0:WINDOW, :] = kh_ref[tile:tile + WINDOW, :]
        vh_ref[0:WINDOW, :] = vh_ref[tile:tile + WINDOW, :]
        afull_ref[0:CONV_HIST, :] = afull_ref[tile:tile + CONV_HIST, :]

    band = WINDOW + CHUNK
    key_chunk = lax.broadcasted_iota(jnp.int32, (1, band), 1) // CHUNK

    h = _rms(x_ref[...], nmix_ref[...]).astype(BF16)
    for c0 in range(0, D_MODEL, MXU_WIDTH):
        c1 = c0 + MXU_WIDTH
        lin = _dot(h, wr_ref[:, REST_U + c0:REST_U + c1])
        gate = _dot(h, wr_ref[:, REST_U + D_MODEL + c0:REST_U + D_MODEL + c1])
        afull_ref[CONV_HIST:CONV_HIST + tile, c0:c1] = lin * jax.nn.sigmoid(gate)
        for r in range(0, tile, CONV_ROWS):
            _dwconv(afull_ref, r, wdw_ref, y_ref, r, CONV_ROWS, c0, c1)
    rope_k = (cos_ref[...], sinlo_ref[...], sinhi_ref[...])
    rope_q = tuple(tb * SCORE_SCALE for tb in rope_k)
    q_ref[...] = _rope(_dot(h, wq_ref[...]), *rope_q).astype(BF16)
    k = _rope(_dot(h, wr_ref[:, REST_K:REST_V]), *rope_k)
    v = _dot(h, wr_ref[:, REST_V:REST_U])
    kh_ref[WINDOW:WINDOW + tile, :] = k.astype(BF16)
    vh_ref[WINDOW:WINDOW + tile, :] = v.astype(BF16)
    kkeep_ref[0] = k[tile - WINDOW:, :]
    vkeep_ref[0] = v[tile - WINDOW:, :]
    ckeep_ref[0] = afull_ref[CONV_HIST + tile - CONV_KEEP:CONV_HIST + tile, :]
    g_ref[...] = jax.nn.sigmoid(_dot(h, wr_ref[:, REST_G:]))

    xo_ref[...] = _ffn_tile(xmid_ref[...], nmlp_ref, wup_ref, wdown_ref, nfin_ref, final)
    n_chunks = tile // CHUNK

    conv_act = _conv_tail(y_ref[...], bdw_ref[...], lng_ref[...], lnb_ref[...]).astype(BF16)
    for c in range(n_chunks):
        valid = None
        if c < WINDOW // CHUNK:
            valid = (t * n_chunks + c - WINDOW // CHUNK + key_chunk) >= 0
        o = _attend(q_ref[c * CHUNK:(c + 1) * CHUNK, :], kh_ref[c * CHUNK:c * CHUNK + band, :],
                    vh_ref[c * CHUNK:c * CHUNK + band, :], sink_ref, valid)
        at_ref[c * CHUNK:(c + 1) * CHUNK, :] = o.astype(BF16)

    merged = (g_ref[:, :D_MODEL] * _dot(at_ref[...], wo_ref[...])
              + g_ref[:, D_MODEL:] * _dot(conv_act, wpw2_ref[...]))
    xmid_ref[...] = x_ref[...] + _dot(merged.astype(BF16), wout_ref[...])


def _mixer_sample_kernel(sink_ref, x_ref, cos_ref, sinlo_ref, sinhi_ref,
                         kpast_ref, vpast_ref, cpast_ref,
                         nmix_ref, wq_ref, wr_ref, wo_ref, wdw_ref,
                         bdw_ref, lng_ref, lnb_ref, wpw2_ref, wout_ref,
                         xo_ref, knew_ref, vnew_ref, ckeep_ref,
                         kf_ref, vf_ref, afull_ref, at_ref, y_ref, *, streams, steps):
    x = x_ref[...]
    h = _rms(x, nmix_ref[...]).astype(BF16)
    rope_k = (cos_ref[...], sinlo_ref[...], sinhi_ref[...])
    rope_q = tuple(tb * SCORE_SCALE for tb in rope_k)
    q = _rope(_dot(h, wq_ref[...]), *rope_q).astype(BF16)
    k = _rope(_dot(h, wr_ref[:, REST_K:REST_V]), *rope_k)
    v = _dot(h, wr_ref[:, REST_V:REST_U])
    knew_ref[...] = k
    vnew_ref[...] = v
    u = _dot(h, wr_ref[:, REST_U:REST_G])
    a = u[:, :D_MODEL] * jax.nn.sigmoid(u[:, D_MODEL:])

    for s in range(streams):
        rows = slice(s * steps, (s + 1) * steps)
        kf_ref[0:WINDOW, :] = kpast_ref[s].astype(BF16)
        vf_ref[0:WINDOW, :] = vpast_ref[s].astype(BF16)
        kf_ref[WINDOW:WINDOW + steps, :] = k[rows].astype(BF16)
        vf_ref[WINDOW:WINDOW + steps, :] = v[rows].astype(BF16)
        at_ref[rows, :] = _attend(q[rows], kf_ref[...], vf_ref[...], sink_ref, None).astype(BF16)

        afull_ref[0:CONV_HIST, :] = cpast_ref[s]
        afull_ref[CONV_HIST:CONV_HIST + steps, :] = a[rows]
        ckeep_ref[s] = afull_ref[CONV_HIST + steps - CONV_KEEP:CONV_HIST + steps, :]
        _dwconv(afull_ref, 0, wdw_ref, y_ref, s * steps, steps)

    conv_act = _conv_tail(y_ref[...], bdw_ref[...], lng_ref[...], lnb_ref[...]).astype(BF16)
    xo_ref[...] = _merge_out(x, h, at_ref[...], conv_act, wr_ref, wo_ref, wpw2_ref, wout_ref)


def _ffn_tile(x, nmlp_ref, wup_ref, wdown_ref, nfin_ref, final):
    hm = _rms(x, nmlp_ref[...]).astype(BF16)
    acc = x
    for c in range(0, D_FF, D_MODEL):
        r = jnp.maximum(_dot(hm, wup_ref[:, c:c + D_MODEL]), 0.0)
        acc = acc + _dot((r * r).astype(BF16), wdown_ref[c:c + D_MODEL, :])
    return _rms(acc, nfin_ref[...]) if final else acc


def _ffn_kernel(x_ref, nmlp_ref, wup_ref, wdown_ref, nfin_ref, o_ref, *, final):
    o_ref[...] = _ffn_tile(x_ref[...], nmlp_ref, wup_ref, wdown_ref, nfin_ref, final)


def _const_spec(shape):
    nd = len(shape)
    return pl.BlockSpec(shape, lambda *_: (0,) * nd, pipeline_mode=pl.Buffered(1))


def _layer_spec(shape, layer):
    nd = len(shape)
    return pl.BlockSpec((None,) + shape, lambda *_: (layer,) + (0,) * nd, pipeline_mode=pl.Buffered(1))


_SMEM_SPEC = pl.BlockSpec(memory_space=pltpu.SMEM)


def _mixer_weight_specs(layer):
    return [
        _layer_spec((1, D_MODEL), layer),
        _layer_spec((D_MODEL, D_MODEL), layer),
        _layer_spec((D_MODEL, REST_WIDTH), layer),
        _layer_spec((D_MODEL, D_MODEL), layer),
        _layer_spec((CONV_W, D_MODEL), layer),
        _layer_spec((1, D_MODEL), layer),
        _layer_spec((1, D_MODEL), layer),
        _layer_spec((1, D_MODEL), layer),
        _layer_spec((D_MODEL, D_MODEL), layer),
        _layer_spec((D_MODEL, D_MODEL), layer),
    ]


def _ffn_weight_specs(layer):
    return [_layer_spec((1, D_MODEL), layer), _layer_spec((D_MODEL, D_FF), layer),
            _layer_spec((D_FF, D_MODEL), layer), _const_spec((1, D_MODEL))]


def _layer_prompt(x, tabs, sink, weights, ffn_weights, layer, final):
    batch, seq, _ = x.shape
    tile = PROMPT_TILE
    tiles_per_seq = seq // tile
    n_tiles = batch * tiles_per_seq
    mixer_tile = lambda s: jnp.minimum(s, n_tiles - 1)
    ffn_tile = lambda s: jnp.maximum(s - 1, 0)
    x_spec = pl.BlockSpec((tile, D_MODEL), lambda s: (mixer_tile(s), 0))
    out_spec = pl.BlockSpec((tile, D_MODEL), lambda s: (ffn_tile(s), 0))
    tab_spec = pl.BlockSpec((tile, LANES), lambda s: (mixer_tile(s) % tiles_per_seq, 0))
    keep_spec = lambda rows, width: pl.BlockSpec(
        (1, rows, width), lambda s: (mixer_tile(s) // tiles_per_seq, 0, 0))
    y, kk, vk, ck = pl.pallas_call(
        functools.partial(_layer_prompt_kernel, tile=tile,
                          tiles_per_seq=tiles_per_seq, n_tiles=n_tiles, final=final),
        grid=(n_tiles + 1,),
        in_specs=([_SMEM_SPEC, x_spec] + [tab_spec] * 3 + _mixer_weight_specs(layer)
                  + _ffn_weight_specs(layer)),
        out_specs=[out_spec, keep_spec(WINDOW, KV_WIDTH), keep_spec(WINDOW, KV_WIDTH),
                   keep_spec(CONV_KEEP, D_MODEL)],
        out_shape=[jax.ShapeDtypeStruct((batch * seq, D_MODEL), F32),
                   jax.ShapeDtypeStruct((batch, WINDOW, KV_WIDTH), F32),
                   jax.ShapeDtypeStruct((batch, WINDOW, KV_WIDTH), F32),
                   jax.ShapeDtypeStruct((batch, CONV_KEEP, D_MODEL), F32)],
        scratch_shapes=[pltpu.VMEM((WINDOW + tile, KV_WIDTH), BF16),
                        pltpu.VMEM((WINDOW + tile, KV_WIDTH), BF16),
                        pltpu.VMEM((CONV_HIST + tile, D_MODEL), F32),
                        pltpu.VMEM((tile, D_MODEL), BF16),
                        pltpu.VMEM((tile, D_MODEL), BF16),
                        pltpu.VMEM((tile, D_MODEL), F32),
                        pltpu.VMEM((tile, 2 * D_MODEL), F32),
                        pltpu.VMEM((tile, D_MODEL), F32)],
        compiler_params=pltpu.CompilerParams(
            dimension_semantics=("arbitrary",), vmem_limit_bytes=VMEM_LIMIT),
        name="layer_prompt",
    )(sink, x.reshape(batch * seq, D_MODEL), *tabs, *weights, *ffn_weights)
    return y.reshape(batch, seq, D_MODEL), kk, vk, ck


def _mixer_sample(x, tabs, kpast, vpast, cpast, sink, weights, layer):
    streams, steps, _ = x.shape
    rows = streams * steps
    full = lambda shape: pl.BlockSpec(shape, lambda i: (0,) * len(shape))
    return pl.pallas_call(
        functools.partial(_mixer_sample_kernel, streams=streams, steps=steps),
        grid=(1,),
        in_specs=[_SMEM_SPEC, full((rows, D_MODEL))] + [full((rows, LANES))] * 3
        + [full((streams, WINDOW, KV_WIDTH)), full((streams, WINDOW, KV_WIDTH)),
           full((streams, CONV_HIST, D_MODEL))] + _mixer_weight_specs(layer),
        out_specs=[full((rows, D_MODEL)), full((rows, KV_WIDTH)), full((rows, KV_WIDTH)),
                   full((streams, CONV_KEEP, D_MODEL))],
        out_shape=[jax.ShapeDtypeStruct((rows, D_MODEL), F32),
                   jax.ShapeDtypeStruct((rows, KV_WIDTH), F32),
                   jax.ShapeDtypeStruct((rows, KV_WIDTH), F32),
                   jax.ShapeDtypeStruct((streams, CONV_KEEP, D_MODEL), F32)],
        scratch_shapes=[pltpu.VMEM((WINDOW + steps, KV_WIDTH), BF16),
                        pltpu.VMEM((WINDOW + steps, KV_WIDTH), BF16),
                        pltpu.VMEM((CONV_HIST + steps, D_MODEL), F32),
                        pltpu.VMEM((rows, D_MODEL), BF16),
                        pltpu.VMEM((rows, D_MODEL), F32)],
        compiler_params=pltpu.CompilerParams(
            dimension_semantics=("arbitrary",), vmem_limit_bytes=VMEM_LIMIT),
        name="mixer_sample",
    )(sink, x.reshape(rows, D_MODEL), *tabs, kpast, vpast, cpast, *weights)


def _ffn(x, nmlp, wup, wdown, nfin, layer, final, name):
    rows = x.shape[0]
    tile = min(FFN_TILE, rows)
    row_spec = pl.BlockSpec((tile, D_MODEL), lambda i: (i, 0))
    return pl.pallas_call(
        functools.partial(_ffn_kernel, final=final),
        grid=(rows // tile,),
        in_specs=[row_spec] + _ffn_weight_specs(layer),
        out_specs=row_spec,
        out_shape=jax.ShapeDtypeStruct((rows, D_MODEL), F32),
        compiler_params=pltpu.CompilerParams(
            dimension_semantics=("arbitrary",), vmem_limit_bytes=VMEM_LIMIT),
        name=name,
    )(x, nmlp, wup, wdown, nfin)


def _rope_tables(pos):
    half = HEAD_DIM // 2
    inv_freq = ROPE_THETA ** (-jnp.arange(half, dtype=F32) / half)
    ang = pos.astype(F32)[:, None] * inv_freq[None, :]
    cos, sin = jnp.cos(ang), jnp.sin(ang)
    zero = jnp.zeros_like(sin)
    reps = LANES // HEAD_DIM
    cos_t = jnp.tile(jnp.concatenate([cos, cos], axis=1), (1, reps))
    sin_lo = jnp.tile(jnp.concatenate([-sin, zero], axis=1), (1, reps))
    sin_hi = jnp.tile(jnp.concatenate([zero, sin], axis=1), (1, reps))
    return cos_t, sin_lo, sin_hi


def _slot_major(w, axis):
    shape = w.shape
    w = w.reshape(shape[:axis] + (N_KV_HEADS, GROUP, HEAD_DIM) + shape[axis + 1:])
    return jnp.swapaxes(w, axis, axis + 1).reshape(shape)


def kernel(x_prompt, x_sample, cache_k, cache_v, state_conv, norm_mix, w_in, sinks, w_o_attn,
           w_dw, b_dw, ln_conv_g, ln_conv_b, w_pw2, w_out, norm_mlp, w_up, w_down, norm_final):
    batch, seq, _ = x_prompt.shape
    streams, steps, _ = x_sample.shape
    aw = N_Q_HEADS * HEAD_DIM
    tabs_p = _rope_tables(jnp.arange(seq, dtype=jnp.int32))
    tabs_s = tuple(jnp.tile(tb, (streams, 1))
                   for tb in _rope_tables(PAST_LEN + jnp.arange(steps, dtype=jnp.int32)))
    rows = lambda p: p.reshape(p.shape[0], 1, -1)

    weights = (rows(norm_mix), _slot_major(w_in[:, :, :aw], 2).astype(BF16), w_in[:, :, aw:].astype(BF16),
               _slot_major(w_o_attn, 1).astype(BF16), w_dw, rows(b_dw), rows(ln_conv_g), rows(ln_conv_b),
               w_pw2.astype(BF16), w_out.astype(BF16))
    ffn_w = (rows(norm_mlp), w_up.astype(BF16), w_down.astype(BF16), norm_final.reshape(1, -1))

    xp = x_prompt
    xs = x_sample.reshape(streams * steps, D_MODEL)
    outs = [[] for _ in range(6)]
    for l in range(DEPTH):
        final = l == DEPTH - 1

        xp, kp, vp, cp = _layer_prompt(xp, tabs_p, sinks[l], weights, ffn_w, l, final)

        cpast = jnp.pad(state_conv[l], ((0, 0), (CONV_PAD, 0), (0, 0)))
        xs, kn, vn, cn = _mixer_sample(
            xs.reshape(streams, steps, D_MODEL), tabs_s,
            cache_k[l].reshape(streams, WINDOW, KV_WIDTH), cache_v[l].reshape(streams, WINDOW, KV_WIDTH),
            cpast, sinks[l], weights, l)
        xs = _ffn(xs, *ffn_w, l, final, "ffn_sample")

        heads = lambda z, n: z.reshape(-1, n, N_KV_HEADS, HEAD_DIM)
        for dst, val in zip(outs, (heads(kp, WINDOW), heads(vp, WINDOW), cp,
                                   heads(kn, steps), heads(vn, steps), cn)):
            dst.append(val)

    return (xp, xs.reshape(streams, steps, D_MODEL)) + tuple(jnp.stack(o) for o in outs)
```
